```python
import jax, jax.numpy as jnp
from jax import lax
import numpy as np

D_MODEL = 2048
BATCH = 8
SEQ = 4096
DEPTH = 1
DEC_BATCH = 16
DEC_SEQ = 64
PAST_LEN = 1024

CHUNK = 64
SGU_CHUNK = 128
SGU_WIDTH = 1024
SGU_GROUPS = 4
SGU_GDIM = SGU_WIDTH // SGU_GROUPS
POOL_WIDTH = 1024
POOL_WINDOWS = (2, 4, 8, 16)
POOL_GROUPS = len(POOL_WINDOWS)
POOL_GDIM = POOL_WIDTH // POOL_GROUPS
POOL_STATE = max(POOL_WINDOWS) - 1
N_MEM = 256
MEM_HEADS = 4
MEM_HDIM = 256
MEM_WIDTH = MEM_HEADS * MEM_HDIM
N_BRANCH = 3
D_FF = 5632
EPS = 1e-6
OFF_U = 0
OFF_V = SGU_WIDTH
OFF_B = 2 * SGU_WIDTH
OFF_Q = OFF_B + POOL_WIDTH
OFF_G = OFF_Q + MEM_WIDTH
IN_COLS = OFF_G + N_BRANCH * D_MODEL

kernel_name = 'gated_sgu_pool_memxattn_streaming_step'


def rmsnorm(x, g):
    xf = x.astype(jnp.float32)
    y = xf * lax.rsqrt(jnp.mean(xf * xf, axis=-1, keepdims=True) + EPS)
    return (y * g.astype(jnp.float32)).astype(x.dtype)


def spatial_gating(u, v, w_s, b_s, g_v, chunk_len):
    bsz, length, _ = v.shape
    n = length // chunk_len
    vn = rmsnorm(v, g_v).reshape(bsz, n, chunk_len, SGU_GROUPS, SGU_GDIM)
    idx = jnp.arange(chunk_len)
    mask = (idx[None, :] // CHUNK) <= (idx[:, None] // CHUNK)
    ws = jnp.where(mask[None], w_s[:, :chunk_len, :chunk_len], jnp.zeros((), w_s.dtype))
    mixed = jnp.einsum('gij,bnjgd->bnigd', ws, vn)
    mixed = mixed + jnp.transpose(b_s[:, :chunk_len])[None, None, :, :, None]
    return u * mixed.reshape(bsz, length, SGU_WIDTH)


def multiscale_pool(xb, prefix, pos0, w_pool, pool_scale):
    bsz, length, _ = xb.shape
    xp = jnp.concatenate([prefix.astype(xb.dtype), xb], axis=1)
    cs = jnp.cumsum(xp.astype(jnp.float32), axis=1)
    cs = jnp.concatenate([jnp.zeros((bsz, 1, POOL_WIDTH), jnp.float32), cs], axis=1)
    pos = pos0 + jnp.arange(length)
    xf = xb.astype(jnp.float32)
    outs = []
    for g, w in enumerate(POOL_WINDOWS):
        lo, hi = g * POOL_GDIM, (g + 1) * POOL_GDIM
        s = cs[:, POOL_STATE + 1:, lo:hi] - cs[:, POOL_STATE + 1 - w:POOL_STATE + 1 - w + length, lo:hi]
        cnt = jnp.minimum(pos + 1, w).astype(jnp.float32)
        pooled = s / cnt[None, :, None] - xf[..., lo:hi]
        outs.append(jnp.einsum('blc,cd->bld', pooled, w_pool[g].astype(jnp.float32)))
    out = jnp.concatenate(outs, axis=-1) * pool_scale.astype(jnp.float32)
    return out.astype(xb.dtype), xp[:, -POOL_STATE:]


def mem_kv(mem, g_mem, w_mk, w_mv):
    bsz = mem.shape[0]
    mn = rmsnorm(mem, g_mem)
    k = (mn @ w_mk).reshape(bsz, N_MEM, MEM_HEADS, MEM_HDIM)
    v = (mn @ w_mv).reshape(bsz, N_MEM, MEM_HEADS, MEM_HDIM)
    return k, v


def mem_attend(q, k, v):
    bsz, length, _ = q.shape
    qh = q.reshape(bsz, length, MEM_HEADS, MEM_HDIM)
    s = jnp.einsum('blhd,bmhd->bhlm', qh, k).astype(jnp.float32) * (MEM_HDIM ** -0.5)
    p = jax.nn.softmax(s, axis=-1).astype(v.dtype)
    o = jnp.einsum('bhlm,bmhd->blhd', p, v)
    return o.reshape(bsz, length, MEM_WIDTH)


def block(x, pool_prefix, pos0, mem_k, mem_v, chunk_len,
          g_mix, w_in, b_gate, g_sgu_v, w_sgu, b_sgu, w_pool, pool_scale,
          w_pa, w_pb, w_pc, w_o, g_ffn, w_ff_gate, w_ff_up, w_ff_down):
    bsz, length, _ = x.shape
    h = rmsnorm(x, g_mix)
    z = h @ w_in
    uv = jax.nn.gelu(z[..., OFF_U:OFF_B])
    u, v = uv[..., :SGU_WIDTH], uv[..., SGU_WIDTH:]
    xb = z[..., OFF_B:OFF_Q]
    q = z[..., OFF_Q:OFF_G]
    gates = jax.nn.sigmoid(z[..., OFF_G:] + b_gate).reshape(bsz, length, N_BRANCH, D_MODEL)
    ya = spatial_gating(u, v, w_sgu, b_sgu, g_sgu_v, chunk_len) @ w_pa
    pooled, new_pool = multiscale_pool(xb, pool_prefix, pos0, w_pool, pool_scale)
    yb = pooled @ w_pb
    yc = mem_attend(q, mem_k, mem_v) @ w_pc
    merged = gates[:, :, 0] * ya + gates[:, :, 1] * yb + gates[:, :, 2] * yc
    x = x + merged @ w_o
    hf = rmsnorm(x, g_ffn)
    x = x + (jax.nn.silu(hf @ w_ff_gate) * (hf @ w_ff_up)) @ w_ff_down
    return x, new_pool, v


def setup_inputs(seed: int = 0) -> dict:
    key = jax.random.key(seed)
    ks = jax.random.split(key, 32)
    f32 = jnp.float32
    def nrm(k, shape, scale=1.0):
        return jax.random.normal(k, shape, f32) * scale
    def gain(k, shape):
        return 1.0 + 0.05 * jax.random.normal(k, shape, f32)
    return {
        'x_prompt': nrm(ks[0], (BATCH, SEQ, D_MODEL)),
        'x_sample': nrm(ks[1], (DEC_BATCH, DEC_SEQ, D_MODEL)),
        'mem_prompt': nrm(ks[2], (BATCH, N_MEM, D_MODEL)),
        'state_pool': nrm(ks[3], (DEPTH, DEC_BATCH, POOL_STATE, POOL_WIDTH)),
        'cache_mem_k': nrm(ks[4], (DEPTH, DEC_BATCH, N_MEM, MEM_HEADS, MEM_HDIM)),
        'cache_mem_v': nrm(ks[5], (DEPTH, DEC_BATCH, N_MEM, MEM_HEADS, MEM_HDIM)),
        'g_mix': gain(ks[6], (DEPTH, D_MODEL)),
        'w_in': nrm(ks[7], (DEPTH, D_MODEL, IN_COLS), D_MODEL ** -0.5),
        'b_gate': nrm(ks[8], (DEPTH, N_BRANCH * D_MODEL), 0.02),
        'g_sgu_v': gain(ks[9], (DEPTH, SGU_WIDTH)),
        'w_sgu': nrm(ks[10], (DEPTH, SGU_GROUPS, SGU_CHUNK, SGU_CHUNK), SGU_CHUNK ** -0.5),
        'b_sgu': gain(ks[11], (DEPTH, SGU_GROUPS, SGU_CHUNK)),
        'w_pool': nrm(ks[12], (DEPTH, POOL_GROUPS, POOL_GDIM, POOL_GDIM), POOL_GDIM ** -0.5),
        'pool_scale': gain(ks[13], (DEPTH, POOL_WIDTH)),
        'g_mem': gain(ks[14], (DEPTH, D_MODEL)),
        'w_mk': nrm(ks[15], (DEPTH, D_MODEL, MEM_WIDTH), D_MODEL ** -0.5),
        'w_mv': nrm(ks[16], (DEPTH, D_MODEL, MEM_WIDTH), D_MODEL ** -0.5),
        'w_pa': nrm(ks[17], (DEPTH, SGU_WIDTH, D_MODEL), SGU_WIDTH ** -0.5),
        'w_pb': nrm(ks[18], (DEPTH, POOL_WIDTH, D_MODEL), POOL_WIDTH ** -0.5),
        'w_pc': nrm(ks[19], (DEPTH, MEM_WIDTH, D_MODEL), MEM_WIDTH ** -0.5),
        'w_o': nrm(ks[20], (DEPTH, D_MODEL, D_MODEL), D_MODEL ** -0.5),
        'g_ffn': gain(ks[21], (DEPTH, D_MODEL)),
        'w_ff_gate': nrm(ks[22], (DEPTH, D_MODEL, D_FF), D_MODEL ** -0.5),
        'w_ff_up': nrm(ks[23], (DEPTH, D_MODEL, D_FF), D_MODEL ** -0.5),
        'w_ff_down': nrm(ks[24], (DEPTH, D_FF, D_MODEL), D_FF ** -0.5),
        'g_final': gain(ks[25], (D_MODEL,)),
    }


def reference(x_prompt, x_sample, mem_prompt, state_pool, cache_mem_k, cache_mem_v,
              g_mix, w_in, b_gate, g_sgu_v, w_sgu, b_sgu, w_pool, pool_scale,
              g_mem, w_mk, w_mv, w_pa, w_pb, w_pc, w_o,
              g_ffn, w_ff_gate, w_ff_up, w_ff_down, g_final):
    xp, xs = x_prompt, x_sample
    dec_seq = x_sample.shape[1]
    pool_p, pool_s, mk_p, mv_p, v_s = [], [], [], [], []
    for l in range(DEPTH):
        lw = (g_mix[l], w_in[l], b_gate[l], g_sgu_v[l], w_sgu[l], b_sgu[l], w_pool[l], pool_scale[l],
              w_pa[l], w_pb[l], w_pc[l], w_o[l], g_ffn[l], w_ff_gate[l], w_ff_up[l], w_ff_down[l])
        mk, mv = mem_kv(mem_prompt, g_mem[l], w_mk[l], w_mv[l])
        zero_prefix = jnp.zeros((xp.shape[0], POOL_STATE, POOL_WIDTH), xp.dtype)
        xp, np_pool, _ = block(xp, zero_prefix, 0, mk, mv, SGU_CHUNK, *lw)
        pool_p.append(np_pool)
        mk_p.append(mk)
        mv_p.append(mv)
        xs, ns_pool, vrows = block(xs, state_pool[l], PAST_LEN, cache_mem_k[l], cache_mem_v[l], dec_seq, *lw)
        pool_s.append(ns_pool)
        v_s.append(vrows)
    y_prompt = rmsnorm(xp, g_final)
    y_sample = rmsnorm(xs, g_final)
    return (y_prompt, y_sample, jnp.stack(pool_p), jnp.stack(pool_s),
            jnp.stack(mk_p), jnp.stack(mv_p), jnp.stack(v_s))
```

```python
import functools

import jax
import jax.numpy as jnp
from jax import lax
from jax.experimental import pallas as pl
from jax.experimental.pallas import tpu as pltpu

F32 = jnp.float32
BF16 = jnp.bfloat16

EPS = 1e-6
CHUNK_SHIFT = 6
SGU_CHUNK = 128
SGU_WIDTH = 1024
SGU_GROUPS = 4
SGU_GDIM = SGU_WIDTH // SGU_GROUPS
POOL_WIDTH = 1024
POOL_WINDOWS = (2, 4, 8, 16)
POOL_GDIM = POOL_WIDTH // len(POOL_WINDOWS)
POOL_STATE = max(POOL_WINDOWS) - 1
N_MEM = 256
MEM_HEADS = 4
MEM_HDIM = 256
MEM_WIDTH = MEM_HEADS * MEM_HDIM
PAST_LEN = 1024

MIB = 1024 * 1024


def _rms(x, gain):
    ms = jnp.mean(x * x, axis=-1, keepdims=True)
    return (x * lax.rsqrt(ms + EPS)) * gain


def _gelu_tanh(x):
    cdf = 0.5 * (1.0 + jnp.tanh(0.7978845608028654 * (x + 0.044715 * (x * x * x))))
    return x * cdf


def _sigmoid(x):
    return 1.0 / (1.0 + jnp.exp(-x))


def _norm_matmul_kernel(*refs, epilogues, has_bias):
    if has_bias:
        x_ref, g_ref, w_ref, b_ref, o_ref, h_ref = refs
    else:
        x_ref, g_ref, w_ref, o_ref, h_ref = refs
        b_ref = None
    n = pl.program_id(1)

    @pl.when(n == 0)
    def _():
        h_ref[...] = _rms(x_ref[...], g_ref[...]).astype(BF16)

    for lo, hi, kind in epilogues:
        @pl.when((n >= lo) & (n < hi))
        def _(kind=kind):
            z = jnp.dot(h_ref[...], w_ref[...], preferred_element_type=F32)
            if kind == "gelu":
                z = _gelu_tanh(z)
            elif kind == "sigmoid_bias":
                z = _sigmoid(z + b_ref[...])
            o_ref[...] = z.astype(o_ref.dtype)


def _norm_matmul(x2, gain, w, bias, epilogues, out_dtype, bm, bn, bias_block0=0, name="norm_matmul"):
    m, k = x2.shape
    n_cols = w.shape[1]
    grid = (m // bm, n_cols // bn)
    in_specs = [
        pl.BlockSpec((bm, k), lambda i, j: (i, 0)),
        pl.BlockSpec((1, k), lambda i, j: (0, 0)),
        pl.BlockSpec((k, bn), lambda i, j: (0, j)),
    ]
    args = [x2, gain.reshape(1, k), w]
    if bias is not None:
        in_specs.append(pl.BlockSpec((1, bn), lambda i, j: (0, jnp.maximum(j - bias_block0, 0))))
        args.append(bias.reshape(1, -1))
    return pl.pallas_call(
        functools.partial(_norm_matmul_kernel, epilogues=epilogues, has_bias=bias is not None),
        grid=grid,
        in_specs=in_specs,
        out_specs=pl.BlockSpec((bm, bn), lambda i, j: (i, j)),
        out_shape=jax.ShapeDtypeStruct((m, n_cols), out_dtype),
        scratch_shapes=[pltpu.VMEM((bm, k), BF16)],
        compiler_params=pltpu.CompilerParams(
            dimension_semantics=("arbitrary", "arbitrary"),
            vmem_limit_bytes=48 * MIB),
        name=name,
    )(*args)


def _mixer_kernel(x_ref, u_ref, v_ref, xb_ref, q_ref, ga_ref, gb_ref, gc_ref, pre_ref,
                  mk_ref, mv_ref, gv_ref, ws_ref, bs_ref, ps_ref, wpool_ref,
                  wpa_ref, wpb_ref, wpc_ref, wo_ref,
                  o_ref, sgu_scr, pool_scr, att_scr,
                  *, n_seq, seq_rows, chunk_len, pool_chunk, pos0, tiles_per_seq, zero_first_prefix):
    i = pl.program_id(0)
    tile_in_seq = i % tiles_per_seq
    tile_rows = n_seq * seq_rows

    ri = lax.broadcasted_iota(jnp.int32, (chunk_len, chunk_len), 0)
    cj = lax.broadcasted_iota(jnp.int32, (chunk_len, chunk_len), 1)
    causal = jnp.right_shift(cj, CHUNK_SHIFT) <= jnp.right_shift(ri, CHUNK_SHIFT)
    ws = [jnp.where(causal, ws_ref[g], 0.0).astype(BF16) for g in range(SGU_GROUPS)]
    bs_full = [jnp.broadcast_to(bs_ref[:, g:g + 1], (chunk_len, SGU_GDIM)) for g in range(SGU_GROUPS)]

    bt = lax.broadcasted_iota(jnp.int32, (pool_chunk, 2 * pool_chunk), 0)
    bj = lax.broadcasted_iota(jnp.int32, (pool_chunk, 2 * pool_chunk), 1)
    back = bt + pool_chunk - bj
    bands = [jnp.where(back >= 0, jnp.where(back < w, 1.0, 0.0), 0.0).astype(BF16) for w in POOL_WINDOWS]
    trow = lax.broadcasted_iota(jnp.int32, (pool_chunk, 1), 0)

    keep_prefix = jnp.where(tile_in_seq == 0, 0.0, 1.0) if zero_first_prefix else None

    for s in range(n_seq):
        r0 = s * seq_rows

        v = v_ref[r0:r0 + seq_rows, :].astype(F32)
        vn = _rms(v, gv_ref[...]).astype(BF16)
        for c in range(seq_rows // chunk_len):
            c0 = c * chunk_len
            for g in range(SGU_GROUPS):
                lo, hi = g * SGU_GDIM, (g + 1) * SGU_GDIM
                mixed = jnp.dot(ws[g], vn[c0:c0 + chunk_len, lo:hi], preferred_element_type=F32) + bs_full[g]
                u = u_ref[r0 + c0:r0 + c0 + chunk_len, lo:hi].astype(F32)
                sgu_scr[r0 + c0:r0 + c0 + chunk_len, lo:hi] = (u * mixed).astype(BF16)

        for c in range(seq_rows // pool_chunk):
            c0 = r0 + c * pool_chunk
            pos = pos0 + tile_in_seq * tile_rows + c * pool_chunk + trow
            for g, w in enumerate(POOL_WINDOWS):
                lo, hi = g * POOL_GDIM, (g + 1) * POOL_GDIM
                cur = xb_ref[c0:c0 + pool_chunk, lo:hi]
                if c == 0:
                    prev = pre_ref[s * pool_chunk:(s + 1) * pool_chunk, lo:hi]
                    if zero_first_prefix:
                        prev = (prev.astype(F32) * keep_prefix).astype(BF16)
                else:
                    prev = xb_ref[c0 - pool_chunk:c0, lo:hi]
                window = jnp.concatenate([prev, cur], axis=0)
                wsum = jnp.dot(bands[g], window, preferred_element_type=F32)
                cnt = jnp.minimum(pos + 1, w).astype(F32)
                pooled = wsum / cnt - cur.astype(F32)
                pg = jnp.dot(pooled.astype(BF16), wpool_ref[g], preferred_element_type=F32)
                pool_scr[c0:c0 + pool_chunk, lo:hi] = (pg * ps_ref[:, lo:hi]).astype(BF16)

        kb = mk_ref[s].astype(BF16)
        vb = mv_ref[s].astype(BF16)
        for h in range(MEM_HEADS):
            lo, hi = h * MEM_HDIM, (h + 1) * MEM_HDIM
            qh = q_ref[r0:r0 + seq_rows, lo:hi]
            sc = lax.dot_general(qh, kb[:, lo:hi], (((1,), (1,)), ((), ())),
                                 preferred_element_type=F32) * (MEM_HDIM ** -0.5)
            e = jnp.exp(sc - jnp.max(sc, axis=-1, keepdims=True))
            p = (e / jnp.sum(e, axis=-1, keepdims=True)).astype(BF16)
            att_scr[r0:r0 + seq_rows, lo:hi] = jnp.dot(
                p, vb[:, lo:hi], preferred_element_type=F32).astype(BF16)

    merged = ga_ref[...].astype(F32) * jnp.dot(sgu_scr[...], wpa_ref[...], preferred_element_type=F32)
    merged += gb_ref[...].astype(F32) * jnp.dot(pool_scr[...], wpb_ref[...], preferred_element_type=F32)
    merged += gc_ref[...].astype(F32) * jnp.dot(att_scr[...], wpc_ref[...], preferred_element_type=F32)
    o_ref[...] = x_ref[...] + jnp.dot(merged.astype(BF16), wo_ref[...], preferred_element_type=F32)


def _mixer(x2, z, prefix, prefix_in_z, mem_k, mem_v, gv, ws, bs_t, pscale, wpool, wpa, wpb, wpc, wo,
           *, n_seq, seq_rows, chunk_len, pos0, tiles_per_seq, name):
    m, d = x2.shape
    tm = n_seq * seq_rows
    pool_chunk = min(SGU_CHUNK, seq_rows)
    grid = (m // tm,)
    const2 = lambda i: (0, 0)
    const3 = lambda i: (0, 0, 0)
    single = pl.Buffered(1)
    if prefix_in_z:
        blocks_per_tile = tm // pool_chunk
        pre_spec = pl.BlockSpec((pool_chunk, POOL_WIDTH),
                                lambda i: (jnp.maximum(i * blocks_per_tile - 1, 0), 2))
        mem_idx = lambda i: (i // tiles_per_seq, 0, 0)
    else:
        pre_spec = pl.BlockSpec((n_seq * pool_chunk, POOL_WIDTH), lambda i: (i, 0))
        mem_idx = lambda i: (i, 0, 0)
    in_specs = [
        pl.BlockSpec((tm, d), lambda i: (i, 0)),
        pl.BlockSpec((tm, SGU_WIDTH), lambda i: (i, 0)),
        pl.BlockSpec((tm, SGU_WIDTH), lambda i: (i, 1)),
        pl.BlockSpec((tm, POOL_WIDTH), lambda i: (i, 2)),
        pl.BlockSpec((tm, MEM_WIDTH), lambda i: (i, 3)),
        pl.BlockSpec((tm, d), lambda i: (i, 2)),
        pl.BlockSpec((tm, d), lambda i: (i, 3)),
        pl.BlockSpec((tm, d), lambda i: (i, 4)),
        pre_spec,
        pl.BlockSpec((n_seq, N_MEM, MEM_WIDTH), mem_idx),
        pl.BlockSpec((n_seq, N_MEM, MEM_WIDTH), mem_idx),
        pl.BlockSpec((1, SGU_WIDTH), const2),
        pl.BlockSpec((SGU_GROUPS, chunk_len, chunk_len), const3),
        pl.BlockSpec((chunk_len, SGU_GROUPS), const2),
        pl.BlockSpec((1, POOL_WIDTH), const2),
        pl.BlockSpec((len(POOL_WINDOWS), POOL_GDIM, POOL_GDIM), const3, pipeline_mode=single),
        pl.BlockSpec((SGU_WIDTH, d), const2, pipeline_mode=single),
        pl.BlockSpec((POOL_WIDTH, d), const2, pipeline_mode=single),
        pl.BlockSpec((MEM_WIDTH, d), const2, pipeline_mode=single),
        pl.BlockSpec((d, d), const2, pipeline_mode=single),
    ]
    kern = functools.partial(
        _mixer_kernel, n_seq=n_seq, seq_rows=seq_rows, chunk_len=chunk_len, pool_chunk=pool_chunk,
        pos0=pos0, tiles_per_seq=tiles_per_seq, zero_first_prefix=prefix_in_z)
    return pl.pallas_call(
        kern,
        grid=grid,
        in_specs=in_specs,
        out_specs=pl.BlockSpec((tm, d), lambda i: (i, 0)),
        out_shape=jax.ShapeDtypeStruct((m, d), F32),
        scratch_shapes=[pltpu.VMEM((tm, SGU_WIDTH), BF16),
                        pltpu.VMEM((tm, POOL_WIDTH), BF16),
                        pltpu.VMEM((tm, MEM_WIDTH), BF16)],
        compiler_params=pltpu.CompilerParams(
            dimension_semantics=("arbitrary",),
            vmem_limit_bytes=56 * MIB),
        name=name,
    )(x2, z, z, z, z, z, z, z, prefix, mem_k, mem_v, gv.reshape(1, -1), ws, bs_t,
      pscale.reshape(1, -1), wpool, wpa, wpb, wpc, wo)


def _ffn_kernel(x_ref, g_ref, wg_ref, wu_ref, wd_ref, gf_ref, o_ref, h_ref, *, n_chunks):
    f = pl.program_id(1)

    @pl.when(f == 0)
    def _():
        x = x_ref[...]
        h_ref[...] = _rms(x, g_ref[...]).astype(BF16)
        o_ref[...] = x

    h = h_ref[...]
    gate = jnp.dot(h, wg_ref[...], preferred_element_type=F32)
    up = jnp.dot(h, wu_ref[...], preferred_element_type=F32)
    act = ((gate * _sigmoid(gate)) * up).astype(BF16)
    o_ref[...] += jnp.dot(act, wd_ref[...], preferred_element_type=F32)

    @pl.when(f == n_chunks - 1)
    def _():
        o_ref[...] = _rms(o_ref[...], gf_ref[...])


def _ffn(x2, g_ffn, wg, wu, wd, g_final, *, tm, fc, name):
    m, d = x2.shape
    d_ff = wg.shape[1]
    n_chunks = d_ff // fc
    return pl.pallas_call(
        functools.partial(_ffn_kernel, n_chunks=n_chunks),
        grid=(m // tm, n_chunks),
        in_specs=[
            pl.BlockSpec((tm, d), lambda i, f: (i, 0)),
            pl.BlockSpec((1, d), lambda i, f: (0, 0)),
            pl.BlockSpec((d, fc), lambda i, f: (0, f)),
            pl.BlockSpec((d, fc), lambda i, f: (0, f)),
            pl.BlockSpec((fc, d), lambda i, f: (f, 0)),
            pl.BlockSpec((1, d), lambda i, f: (0, 0)),
        ],
        out_specs=pl.BlockSpec((tm, d), lambda i, f: (i, 0)),
        out_shape=jax.ShapeDtypeStruct((m, d), F32),
        scratch_shapes=[pltpu.VMEM((tm, d), BF16)],
        compiler_params=pltpu.CompilerParams(
            dimension_semantics=("arbitrary", "arbitrary"),
            vmem_limit_bytes=52 * MIB),
        name=name,
    )(x2, g_ffn.reshape(1, d), wg, wu, wd, g_final.reshape(1, d))


IN_EPILOGUES = ((0, 2, "gelu"), (2, 4, "identity"), (4, 10, "sigmoid_bias"))
IN_BLOCK = 1024


def kernel(x_prompt, x_sample, mem_prompt, state_pool, cache_mem_k, cache_mem_v, g_mix, w_in, b_gate, g_sgu_v, w_sgu, b_sgu, w_pool, pool_scale, g_mem, w_mk, w_mv, w_pa, w_pb, w_pc, w_o, g_ffn, w_ff_gate, w_ff_up, w_ff_down, g_final):
    depth = g_mix.shape[0]
    batch, seq, d = x_prompt.shape
    dec_batch, dec_seq, _ = x_sample.shape
    assert depth == 1, "final norm is fused into the last layer's FFN call"

    xp = x_prompt.reshape(batch * seq, d)
    xs = x_sample.reshape(dec_batch * dec_seq, d)
    mem2 = mem_prompt.reshape(batch * N_MEM, d)

    pool_p, pool_s, mk_p, mv_p, v_s = [], [], [], [], []
    for l in range(depth):
        w_in_b = w_in[l].astype(BF16)
        wpool_b = w_pool[l].astype(BF16)
        wpa_b, wpb_b, wpc_b, wo_b = (w[l].astype(BF16) for w in (w_pa, w_pb, w_pc, w_o))
        wg_b, wu_b, wd_b = (w[l].astype(BF16) for w in (w_ff_gate, w_ff_up, w_ff_down))

        mk = _norm_matmul(mem2, g_mem[l], w_mk[l].astype(BF16), None, ((0, 1, "identity"),), F32,
                          1024, MEM_WIDTH, name="mem_k")
        mv = _norm_matmul(mem2, g_mem[l], w_mv[l].astype(BF16), None, ((0, 1, "identity"),), F32,
                          1024, MEM_WIDTH, name="mem_v")
        mk_p.append(mk.reshape(batch, N_MEM, MEM_HEADS, MEM_HDIM))
        mv_p.append(mv.reshape(batch, N_MEM, MEM_HEADS, MEM_HDIM))

        zp = _norm_matmul(xp, g_mix[l], w_in_b, b_gate[l], IN_EPILOGUES, BF16, 1024, IN_BLOCK,
                          bias_block0=4, name="in_proj_prompt")
        tm_p = 256
        x1p = _mixer(xp, zp, zp, True, mk.reshape(batch, N_MEM, MEM_WIDTH), mv.reshape(batch, N_MEM, MEM_WIDTH),
                     g_sgu_v[l], w_sgu[l], jnp.transpose(b_sgu[l]), pool_scale[l], wpool_b,
                     wpa_b, wpb_b, wpc_b, wo_b,
                     n_seq=1, seq_rows=tm_p, chunk_len=SGU_CHUNK, pos0=0, tiles_per_seq=seq // tm_p,
                     name="mixer_prompt")
        xp = _ffn(x1p, g_ffn[l], wg_b, wu_b, wd_b, g_final, tm=512, fc=512, name="ffn_prompt")
        xb_p = zp.reshape(batch, seq, -1)[:, seq - POOL_STATE:, 2 * SGU_WIDTH:2 * SGU_WIDTH + POOL_WIDTH]
        pool_p.append(xb_p.astype(F32))

        zs = _norm_matmul(xs, g_mix[l], w_in_b, b_gate[l], IN_EPILOGUES, BF16, 1024, IN_BLOCK,
                          bias_block0=4, name="in_proj_sample")
        prefix = jnp.pad(state_pool[l], ((0, 0), (dec_seq - POOL_STATE, 0), (0, 0)))
        prefix = prefix.reshape(dec_batch * dec_seq, POOL_WIDTH).astype(BF16)
        x1s = _mixer(xs, zs, prefix, False,
                     cache_mem_k[l].reshape(dec_batch, N_MEM, MEM_WIDTH),
                     cache_mem_v[l].reshape(dec_batch, N_MEM, MEM_WIDTH),
                     g_sgu_v[l], w_sgu[l][:, :dec_seq, :dec_seq], jnp.transpose(b_sgu[l][:, :dec_seq]),
                     pool_scale[l], wpool_b, wpa_b, wpb_b, wpc_b, wo_b,
                     n_seq=2, seq_rows=dec_seq, chunk_len=dec_seq, pos0=PAST_LEN, tiles_per_seq=1,
                     name="mixer_sample")
        xs = _ffn(x1s, g_ffn[l], wg_b, wu_b, wd_b, g_final, tm=512, fc=512, name="ffn_sample")
        zs3 = zs.reshape(dec_batch, dec_seq, -1)
        pool_s.append(zs3[:, dec_seq - POOL_STATE:, 2 * SGU_WIDTH:2 * SGU_WIDTH + POOL_WIDTH].astype(F32))
        v_s.append(zs3[:, :, SGU_WIDTH:2 * SGU_WIDTH].astype(F32))

    y_prompt = xp.reshape(batch, seq, d)
    y_sample = xs.reshape(dec_batch, dec_seq, d)
    return (y_prompt, y_sample, jnp.stack(pool_p), jnp.stack(pool_s),
            jnp.stack(mk_p), jnp.stack(mv_p), jnp.stack(v_s))
```

```python
import functools

import jax
import jax.numpy as jnp
from jax import lax
from jax.experimental import pallas as pl
from jax.experimental.pallas import tpu as pltpu

F32 = jnp.float32
BF16 = jnp.bfloat16

EPS = 1e-6
CHUNK_SHIFT = 6
SGU_CHUNK = 128
SGU_WIDTH = 1024
SGU_GROUPS = 4
SGU_GDIM = SGU_WIDTH // SGU_GROUPS
POOL_WIDTH = 1024
POOL_WINDOWS = (2, 4, 8, 16)
POOL_GDIM = POOL_WIDTH // len(POOL_WINDOWS)
POOL_STATE = max(POOL_WINDOWS) - 1
N_MEM = 256
MEM_HEADS = 4
MEM_HDIM = 256
MEM_WIDTH = MEM_HEADS * MEM_HDIM
PAST_LEN = 1024

MIB = 1024 * 1024

IN_ROWS = 1024
IN_COLS_BLOCK = 2048
MIXER_ROWS = 256
MIXER_SAMPLE_SEQS = 2
FFN_ROWS = 512
FFN_COLS = 512
ROW_CHUNK = 512
IN_ROW_CHUNK = 256


def _rms(x, gain):
    ms = jnp.mean(x * x, axis=-1, keepdims=True)
    return (x * lax.rsqrt(ms + EPS)) * gain


def _gelu_tanh(x):
    cdf = 0.5 * (1.0 + jnp.tanh(0.7978845608028654 * (x + 0.044715 * (x * x * x))))
    return x * cdf


def _sigmoid(x):
    return 0.5 + 0.5 * jnp.tanh(0.5 * x)


def _row_chunks(rows, chunk=ROW_CHUNK):
    step = min(chunk, rows)
    return [(r, step) for r in range(0, rows, step)]


def _norm_matmul_kernel(*refs, epilogues, has_bias):
    if has_bias:
        x_ref, g_ref, w_ref, b_ref, o_ref, h_ref = refs
    else:
        x_ref, g_ref, w_ref, o_ref, h_ref = refs
        b_ref = None
    n = pl.program_id(1)

    def block(kind, with_norm):
        for r0, rows in _row_chunks(x_ref.shape[0], IN_ROW_CHUNK):
            rs = slice(r0, r0 + rows)
            if with_norm:
                h_ref[rs, :] = _rms(x_ref[rs, :], g_ref[...]).astype(BF16)
            z = jnp.dot(h_ref[rs, :], w_ref[...], preferred_element_type=F32)
            if kind == "gelu":
                z = _gelu_tanh(z)
            elif kind == "sigmoid_bias":
                z = _sigmoid(z + b_ref[...])
            o_ref[rs, :] = z.astype(o_ref.dtype)

    for lo, hi, kind in epilogues:
        if lo == 0:
            pl.when(n == 0)(functools.partial(block, kind, True))
            lo = 1
        if hi > lo:
            pl.when((n >= lo) & (n < hi))(functools.partial(block, kind, False))


def _norm_matmul(x2, gain, w, bias, epilogues, out_dtype, bn, bias_block0=0, name="norm_matmul"):
    m, k = x2.shape
    n_cols = w.shape[1]
    bm = min(IN_ROWS, m)
    grid = (m // bm, n_cols // bn)
    in_specs = [
        pl.BlockSpec((bm, k), lambda i, j: (i, 0)),
        pl.BlockSpec((1, k), lambda i, j: (0, 0)),
        pl.BlockSpec((k, bn), lambda i, j: (0, j)),
    ]
    args = [x2, gain.reshape(1, k), w]
    if bias is not None:
        in_specs.append(pl.BlockSpec((1, bn), lambda i, j: (0, jnp.maximum(j - bias_block0, 0))))
        args.append(bias.reshape(1, -1))
    return pl.pallas_call(
        functools.partial(_norm_matmul_kernel, epilogues=epilogues, has_bias=bias is not None),
        grid=grid,
        in_specs=in_specs,
        out_specs=pl.BlockSpec((bm, bn), lambda i, j: (i, j)),
        out_shape=jax.ShapeDtypeStruct((m, n_cols), out_dtype),
        scratch_shapes=[pltpu.VMEM((bm, k), BF16)],
        compiler_params=pltpu.CompilerParams(
            dimension_semantics=("arbitrary", "arbitrary"),
            vmem_limit_bytes=56 * MIB),
        name=name,
    )(*args)


def _mixer_kernel(x_ref, u_ref, v_ref, xb_ref, q_ref, ga_ref, gb_ref, gc_ref, pre_ref,
                  mk_ref, mv_ref, gv_ref, ws_ref, bs_ref, ps_ref, wpool_ref,
                  wpa_ref, wpb_ref, wpc_ref, wo_ref,
                  o_ref, sgu_scr, pool_scr, att_scr,
                  *, n_seq, seq_rows, chunk_len, pool_chunk, pos0, tiles_per_seq, zero_first_prefix):
    i = pl.program_id(0)
    tile_in_seq = i % tiles_per_seq
    tile_rows = n_seq * seq_rows

    ri = lax.broadcasted_iota(jnp.int32, (chunk_len, chunk_len), 0)
    cj = lax.broadcasted_iota(jnp.int32, (chunk_len, chunk_len), 1)
    causal = jnp.right_shift(cj, CHUNK_SHIFT) <= jnp.right_shift(ri, CHUNK_SHIFT)
    ws = [jnp.where(causal, ws_ref[g], 0.0).astype(BF16) for g in range(SGU_GROUPS)]
    bs_full = [jnp.broadcast_to(bs_ref[:, g:g + 1], (chunk_len, SGU_GDIM)) for g in range(SGU_GROUPS)]

    bt = lax.broadcasted_iota(jnp.int32, (pool_chunk, 2 * pool_chunk), 0)
    bj = lax.broadcasted_iota(jnp.int32, (pool_chunk, 2 * pool_chunk), 1)
    back = bt + pool_chunk - bj
    bands = [jnp.where(back >= 0, jnp.where(back < w, 1.0, 0.0), 0.0).astype(BF16) for w in POOL_WINDOWS]
    trow = lax.broadcasted_iota(jnp.int32, (pool_chunk, 1), 0)

    keep_prefix = jnp.where(tile_in_seq == 0, 0.0, 1.0) if zero_first_prefix else None

    for s in range(n_seq):
        r0 = s * seq_rows

        v = v_ref[r0:r0 + seq_rows, :].astype(F32)
        vn = _rms(v, gv_ref[...]).astype(BF16)
        for c in range(seq_rows // chunk_len):
            c0 = c * chunk_len
            for g in range(SGU_GROUPS):
                lo, hi = g * SGU_GDIM, (g + 1) * SGU_GDIM
                mixed = jnp.dot(ws[g], vn[c0:c0 + chunk_len, lo:hi], preferred_element_type=F32) + bs_full[g]
                u = u_ref[r0 + c0:r0 + c0 + chunk_len, lo:hi].astype(F32)
                sgu_scr[r0 + c0:r0 + c0 + chunk_len, lo:hi] = (u * mixed).astype(BF16)

        for c in range(seq_rows // pool_chunk):
            c0 = r0 + c * pool_chunk
            pos = pos0 + tile_in_seq * tile_rows + c * pool_chunk + trow
            for g, w in enumerate(POOL_WINDOWS):
                lo, hi = g * POOL_GDIM, (g + 1) * POOL_GDIM
                cur = xb_ref[c0:c0 + pool_chunk, lo:hi]
                if c == 0:
                    prev = pre_ref[s * pool_chunk:(s + 1) * pool_chunk, lo:hi]
                    if zero_first_prefix:
                        prev = (prev.astype(F32) * keep_prefix).astype(BF16)
                else:
                    prev = xb_ref[c0 - pool_chunk:c0, lo:hi]
                window = jnp.concatenate([prev, cur], axis=0)
                wsum = jnp.dot(bands[g], window, preferred_element_type=F32)
                cnt = jnp.minimum(pos + 1, w).astype(F32)
                pooled = wsum / cnt - cur.astype(F32)
                pg = jnp.dot(pooled.astype(BF16), wpool_ref[g], preferred_element_type=F32)
                pool_scr[c0:c0 + pool_chunk, lo:hi] = (pg * ps_ref[:, lo:hi]).astype(BF16)

        kb = mk_ref[s].astype(BF16)
        vb = mv_ref[s].astype(BF16)
        for h in range(MEM_HEADS):
            lo, hi = h * MEM_HDIM, (h + 1) * MEM_HDIM
            qh = q_ref[r0:r0 + seq_rows, lo:hi]
            sc = lax.dot_general(qh, kb[:, lo:hi], (((1,), (1,)), ((), ())),
                                 preferred_element_type=F32) * (MEM_HDIM ** -0.5)
            e = jnp.exp(sc - jnp.max(sc, axis=-1, keepdims=True))
            p = (e / jnp.sum(e, axis=-1, keepdims=True)).astype(BF16)
            att_scr[r0:r0 + seq_rows, lo:hi] = jnp.dot(
                p, vb[:, lo:hi], preferred_element_type=F32).astype(BF16)

    merged = ga_ref[...].astype(F32) * jnp.dot(sgu_scr[...], wpa_ref[...], preferred_element_type=F32)
    merged += gb_ref[...].astype(F32) * jnp.dot(pool_scr[...], wpb_ref[...], preferred_element_type=F32)
    merged += gc_ref[...].astype(F32) * jnp.dot(att_scr[...], wpc_ref[...], preferred_element_type=F32)
    o_ref[...] = x_ref[...] + jnp.dot(merged.astype(BF16), wo_ref[...], preferred_element_type=F32)


def _mixer(x2, z, prefix, prefix_in_z, mem_k, mem_v, gv, ws, bs_t, pscale, wpool, wpa, wpb, wpc, wo,
           *, n_seq, seq_rows, chunk_len, pos0, tiles_per_seq, name):
    m, d = x2.shape
    tm = n_seq * seq_rows
    pool_chunk = min(SGU_CHUNK, seq_rows)
    grid = (m // tm,)
    const2 = lambda i: (0, 0)
    const3 = lambda i: (0, 0, 0)
    single = pl.Buffered(1)
    gate_block0 = (2 * SGU_WIDTH + POOL_WIDTH + MEM_WIDTH) // d
    if prefix_in_z:
        blocks_per_tile = tm // pool_chunk
        pre_spec = pl.BlockSpec((pool_chunk, POOL_WIDTH),
                                lambda i: (jnp.maximum(i * blocks_per_tile - 1, 0), 2))
        mem_idx = lambda i: (i // tiles_per_seq, 0, 0)
    else:
        pre_spec = pl.BlockSpec((n_seq * pool_chunk, POOL_WIDTH), lambda i: (i, 0))
        mem_idx = lambda i: (i, 0, 0)
    in_specs = [
        pl.BlockSpec((tm, d), lambda i: (i, 0)),
        pl.BlockSpec((tm, SGU_WIDTH), lambda i: (i, 0)),
        pl.BlockSpec((tm, SGU_WIDTH), lambda i: (i, 1)),
        pl.BlockSpec((tm, POOL_WIDTH), lambda i: (i, 2)),
        pl.BlockSpec((tm, MEM_WIDTH), lambda i: (i, 3)),
        pl.BlockSpec((tm, d), lambda i: (i, gate_block0)),
        pl.BlockSpec((tm, d), lambda i: (i, gate_block0 + 1)),
        pl.BlockSpec((tm, d), lambda i: (i, gate_block0 + 2)),
        pre_spec,
        pl.BlockSpec((n_seq, N_MEM, MEM_WIDTH), mem_idx),
        pl.BlockSpec((n_seq, N_MEM, MEM_WIDTH), mem_idx),
        pl.BlockSpec((1, SGU_WIDTH), const2),
        pl.BlockSpec((SGU_GROUPS, chunk_len, chunk_len), const3),
        pl.BlockSpec((chunk_len, SGU_GROUPS), const2),
        pl.BlockSpec((1, POOL_WIDTH), const2),
        pl.BlockSpec((len(POOL_WINDOWS), POOL_GDIM, POOL_GDIM), const3, pipeline_mode=single),
        pl.BlockSpec((SGU_WIDTH, d), const2, pipeline_mode=single),
        pl.BlockSpec((POOL_WIDTH, d), const2, pipeline_mode=single),
        pl.BlockSpec((MEM_WIDTH, d), const2, pipeline_mode=single),
        pl.BlockSpec((d, d), const2, pipeline_mode=single),
    ]
    kern = functools.partial(
        _mixer_kernel, n_seq=n_seq, seq_rows=seq_rows, chunk_len=chunk_len, pool_chunk=pool_chunk,
        pos0=pos0, tiles_per_seq=tiles_per_seq, zero_first_prefix=prefix_in_z)
    return pl.pallas_call(
        kern,
        grid=grid,
        in_specs=in_specs,
        out_specs=pl.BlockSpec((tm, d), lambda i: (i, 0)),
        out_shape=jax.ShapeDtypeStruct((m, d), F32),
        scratch_shapes=[pltpu.VMEM((tm, SGU_WIDTH), BF16),
                        pltpu.VMEM((tm, POOL_WIDTH), BF16),
                        pltpu.VMEM((tm, MEM_WIDTH), BF16)],
        compiler_params=pltpu.CompilerParams(
            dimension_semantics=("arbitrary",),
            vmem_limit_bytes=56 * MIB),
        name=name,
    )(x2, z, z, z, z, z, z, z, prefix, mem_k, mem_v, gv.reshape(1, -1), ws, bs_t,
      pscale.reshape(1, -1), wpool, wpa, wpb, wpc, wo)


def _ffn_kernel(x_ref, g_ref, wg_ref, wu_ref, wd_ref, gf_ref, o_ref, h_ref, act_ref, *, n_chunks):
    f = pl.program_id(1)
    slot = f % 2
    chunks = _row_chunks(x_ref.shape[0])

    def activation(rs):
        h = h_ref[rs, :]
        gate = jnp.dot(h, wg_ref[...], preferred_element_type=F32)
        up = jnp.dot(h, wu_ref[...], preferred_element_type=F32)
        act_ref[slot, rs, :] = ((gate * _sigmoid(gate)) * up).astype(BF16)

    @pl.when(f == 0)
    def _():
        for r0, rows in chunks:
            rs = slice(r0, r0 + rows)
            x = x_ref[rs, :]
            h_ref[rs, :] = _rms(x, g_ref[...]).astype(BF16)
            o_ref[rs, :] = x
            activation(rs)

    @pl.when((f > 0) & (f < n_chunks))
    def _():
        activation(slice(None))
        o_ref[...] += jnp.dot(act_ref[1 - slot], wd_ref[...], preferred_element_type=F32)

    @pl.when(f == n_chunks)
    def _():
        for r0, rows in chunks:
            rs = slice(r0, r0 + rows)
            y = o_ref[rs, :] + jnp.dot(act_ref[1 - slot, rs, :], wd_ref[...], preferred_element_type=F32)
            o_ref[rs, :] = _rms(y, gf_ref[...])


def _ffn(x2, g_ffn, wg, wu, wd, g_final, *, name):
    m, d = x2.shape
    d_ff = wg.shape[1]
    tm = min(FFN_ROWS, m)
    fc = FFN_COLS
    n_chunks = d_ff // fc
    up_idx = lambda i, f: (0, jnp.minimum(f, n_chunks - 1))
    return pl.pallas_call(
        functools.partial(_ffn_kernel, n_chunks=n_chunks),
        grid=(m // tm, n_chunks + 1),
        in_specs=[
            pl.BlockSpec((tm, d), lambda i, f: (i, 0)),
            pl.BlockSpec((1, d), lambda i, f: (0, 0)),
            pl.BlockSpec((d, fc), up_idx),
            pl.BlockSpec((d, fc), up_idx),
            pl.BlockSpec((fc, d), lambda i, f: (jnp.maximum(f - 1, 0), 0)),
            pl.BlockSpec((1, d), lambda i, f: (0, 0)),
        ],
        out_specs=pl.BlockSpec((tm, d), lambda i, f: (i, 0)),
        out_shape=jax.ShapeDtypeStruct((m, d), F32),
        scratch_shapes=[pltpu.VMEM((tm, d), BF16), pltpu.VMEM((2, tm, fc), BF16)],
        compiler_params=pltpu.CompilerParams(
            dimension_semantics=("arbitrary", "arbitrary"),
            vmem_limit_bytes=52 * MIB),
        name=name,
    )(x2, g_ffn.reshape(1, d), wg, wu, wd, g_final.reshape(1, d))


IN_EPILOGUES = ((0, 1, "gelu"), (1, 2, "identity"), (2, 5, "sigmoid_bias"))
IN_GATE_BLOCK0 = 2


def kernel(x_prompt, x_sample, mem_prompt, state_pool, cache_mem_k, cache_mem_v, g_mix, w_in, b_gate, g_sgu_v, w_sgu, b_sgu, w_pool, pool_scale, g_mem, w_mk, w_mv, w_pa, w_pb, w_pc, w_o, g_ffn, w_ff_gate, w_ff_up, w_ff_down, g_final):
    depth = g_mix.shape[0]
    batch, seq, d = x_prompt.shape
    dec_batch, dec_seq, _ = x_sample.shape
    assert depth == 1, "final norm is fused into the last layer's FFN call"
    assert d == IN_COLS_BLOCK and dec_seq >= POOL_STATE

    xp = x_prompt.reshape(batch * seq, d)
    xs = x_sample.reshape(dec_batch * dec_seq, d)
    mem2 = mem_prompt.reshape(batch * N_MEM, d)

    pool_p, pool_s, mk_p, mv_p, v_s = [], [], [], [], []
    for l in range(depth):
        w_in_b = w_in[l].astype(BF16)
        wpool_b = w_pool[l].astype(BF16)
        wpa_b, wpb_b, wpc_b, wo_b = (w[l].astype(BF16) for w in (w_pa, w_pb, w_pc, w_o))
        wg_b, wu_b, wd_b = (w[l].astype(BF16) for w in (w_ff_gate, w_ff_up, w_ff_down))

        mk = _norm_matmul(mem2, g_mem[l], w_mk[l].astype(BF16), None, ((0, 1, "identity"),), F32,
                          MEM_WIDTH, name="mem_k")
        mv = _norm_matmul(mem2, g_mem[l], w_mv[l].astype(BF16), None, ((0, 1, "identity"),), F32,
                          MEM_WIDTH, name="mem_v")
        mk_p.append(mk.reshape(batch, N_MEM, MEM_HEADS, MEM_HDIM))
        mv_p.append(mv.reshape(batch, N_MEM, MEM_HEADS, MEM_HDIM))

        zp = _norm_matmul(xp, g_mix[l], w_in_b, b_gate[l], IN_EPILOGUES, BF16, IN_COLS_BLOCK,
                          bias_block0=IN_GATE_BLOCK0, name="in_proj_prompt")
        tm_p = min(MIXER_ROWS, seq)
        x1p = _mixer(xp, zp, zp, True, mk.reshape(batch, N_MEM, MEM_WIDTH), mv.reshape(batch, N_MEM, MEM_WIDTH),
                     g_sgu_v[l], w_sgu[l], jnp.transpose(b_sgu[l]), pool_scale[l], wpool_b,
                     wpa_b, wpb_b, wpc_b, wo_b,
                     n_seq=1, seq_rows=tm_p, chunk_len=SGU_CHUNK, pos0=0, tiles_per_seq=seq // tm_p,
                     name="mixer_prompt")
        xp = _ffn(x1p, g_ffn[l], wg_b, wu_b, wd_b, g_final, name="ffn_prompt")
        xb_p = zp.reshape(batch, seq, -1)[:, seq - POOL_STATE:, 2 * SGU_WIDTH:2 * SGU_WIDTH + POOL_WIDTH]
        pool_p.append(xb_p.astype(F32))

        zs = _norm_matmul(xs, g_mix[l], w_in_b, b_gate[l], IN_EPILOGUES, BF16, IN_COLS_BLOCK,
                          bias_block0=IN_GATE_BLOCK0, name="in_proj_sample")
        prefix = jnp.pad(state_pool[l], ((0, 0), (dec_seq - POOL_STATE, 0), (0, 0)))
        prefix = prefix.reshape(dec_batch * dec_seq, POOL_WIDTH).astype(BF16)
        x1s = _mixer(xs, zs, prefix, False,
                     cache_mem_k[l].reshape(dec_batch, N_MEM, MEM_WIDTH),
                     cache_mem_v[l].reshape(dec_batch, N_MEM, MEM_WIDTH),
                     g_sgu_v[l], w_sgu[l][:, :dec_seq, :dec_seq], jnp.transpose(b_sgu[l][:, :dec_seq]),
                     pool_scale[l], wpool_b, wpa_b, wpb_b, wpc_b, wo_b,
                     n_seq=min(MIXER_SAMPLE_SEQS, dec_batch), seq_rows=dec_seq, chunk_len=dec_seq,
                     pos0=PAST_LEN, tiles_per_seq=1, name="mixer_sample")
        xs = _ffn(x1s, g_ffn[l], wg_b, wu_b, wd_b, g_final, name="ffn_sample")
        zs3 = zs.reshape(dec_batch, dec_seq, -1)
        pool_s.append(zs3[:, dec_seq - POOL_STATE:, 2 * SGU_WIDTH:2 * SGU_WIDTH + POOL_WIDTH].astype(F32))
        v_s.append(zs3[:, :, SGU_WIDTH:2 * SGU_WIDTH].astype(F32))

    y_prompt = xp.reshape(batch, seq, d)
    y_sample = xs.reshape(dec_batch, dec_seq, d)
    return (y_prompt, y_sample, jnp.stack(pool_p), jnp.stack(pool_s),
            jnp.stack(mk_p), jnp.stack(mv_p), jnp.stack(v_s))
```

```python
import functools

import jax
import jax.numpy as jnp
from jax import lax
from jax.experimental import pallas as pl
from jax.experimental.pallas import tpu as pltpu

F32 = jnp.float32
BF16 = jnp.bfloat16

EPS = 1e-6
CHUNK_SHIFT = 6
SGU_CHUNK = 128
SGU_WIDTH = 1024
SGU_GROUPS = 4
SGU_GDIM = SGU_WIDTH // SGU_GROUPS
POOL_WIDTH = 1024
POOL_WINDOWS = (2, 4, 8, 16)
POOL_GDIM = POOL_WIDTH // len(POOL_WINDOWS)
POOL_STATE = max(POOL_WINDOWS) - 1
N_MEM = 256
MEM_HEADS = 4
MEM_HDIM = 256
MEM_WIDTH = MEM_HEADS * MEM_HDIM
PAST_LEN = 1024
OFF_XB = 2 * SGU_WIDTH
OFF_GATES = OFF_XB + POOL_WIDTH + MEM_WIDTH

MIB = 1024 * 1024

IN_ROWS = 1024
IN_COLS = 2048
IN_COLS_CAST = 1024
MIXER_ROWS = 256
MIXER_SAMPLE_SEQS = 2
FFN_ROWS = 1024
FFN_COLS = 512
FFN_COLS_CAST = 256
ROW_CHUNK = 512
VMEM_LIMIT = 57 * MIB


def _rms(x, gain):
    ms = jnp.mean(x * x, axis=-1, keepdims=True)
    return (x * lax.rsqrt(ms + EPS)) * gain


def _gelu_tanh(x):
    cdf = 0.5 * (1.0 + jnp.tanh(0.7978845608028654 * (x + 0.044715 * (x * x * x))))
    return x * cdf


def _sigmoid(x):
    return 0.5 + 0.5 * jnp.tanh(0.5 * x)


def _row_chunks(rows):
    step = min(ROW_CHUNK, rows)
    return [slice(r, r + step) for r in range(0, rows, step)]


def _norm_matmul_kernel(*refs, epilogues, has_bias, cast_w):
    refs = list(refs)
    h_ref = refs.pop()
    wb_ref = refs.pop() if cast_w else None
    o_ref = refs.pop()
    b_ref = refs.pop() if has_bias else None
    x_ref, g_ref, w_ref = refs
    n = pl.program_id(1)

    def block(kind, with_norm):
        if cast_w:
            wb_ref[...] = w_ref[...].astype(BF16)
        w_src = wb_ref if cast_w else w_ref
        for rs in _row_chunks(x_ref.shape[0]):
            if with_norm:
                h_ref[rs, :] = _rms(x_ref[rs, :], g_ref[...]).astype(BF16)
            z = jnp.dot(h_ref[rs, :], w_src[...], preferred_element_type=F32)
            if kind == "gelu":
                z = _gelu_tanh(z)
            elif kind == "sigmoid_bias":
                z = _sigmoid(z + b_ref[...])
            o_ref[rs, :] = z.astype(o_ref.dtype)

    for lo, hi, kind in epilogues:
        if lo == 0:
            pl.when(n == 0)(functools.partial(block, kind, True))
            lo = 1
        if hi > lo:
            pl.when((n >= lo) & (n < hi))(functools.partial(block, kind, False))


def _norm_matmul(x2, gain, w, bias, epilogues, out_dtype, bn, bias_block0=0, cast_w=False, name="norm_matmul"):
    m, k = x2.shape
    n_cols = w.shape[1]
    bm = min(IN_ROWS, m)
    grid = (m // bm, n_cols // bn)
    assert not cast_w or grid[0] == 1, "the bf16 weight copy is written once per column block"
    in_specs = [
        pl.BlockSpec((bm, k), lambda i, j: (i, 0)),
        pl.BlockSpec((1, k), lambda i, j: (0, 0)),
        pl.BlockSpec((k, bn), lambda i, j: (0, j)),
    ]
    args = [x2, gain.reshape(1, k), w]
    if bias is not None:
        in_specs.append(pl.BlockSpec((1, bn), lambda i, j: (0, jnp.maximum(j - bias_block0, 0))))
        args.append(bias.reshape(1, -1))
    out_specs = [pl.BlockSpec((bm, bn), lambda i, j: (i, j))]
    out_shape = [jax.ShapeDtypeStruct((m, n_cols), out_dtype)]
    if cast_w:
        out_specs.append(pl.BlockSpec((k, bn), lambda i, j: (0, j)))
        out_shape.append(jax.ShapeDtypeStruct((k, n_cols), BF16))
    out = pl.pallas_call(
        functools.partial(_norm_matmul_kernel, epilogues=epilogues, has_bias=bias is not None, cast_w=cast_w),
        grid=grid,
        in_specs=in_specs,
        out_specs=out_specs,
        out_shape=out_shape,
        scratch_shapes=[pltpu.VMEM((bm, k), BF16)],
        compiler_params=pltpu.CompilerParams(
            dimension_semantics=("arbitrary", "arbitrary"),
            vmem_limit_bytes=VMEM_LIMIT),
        name=name,
    )(*args)
    return tuple(out) if cast_w else out[0]


def _in_proj(x2, gain, w, bias, cast_w, name):
    bn = IN_COLS_CAST if cast_w else IN_COLS
    epilogues = ((0, OFF_XB // bn, "gelu"), (OFF_XB // bn, OFF_GATES // bn, "identity"),
                 (OFF_GATES // bn, w.shape[1] // bn, "sigmoid_bias"))
    return _norm_matmul(x2, gain, w, bias, epilogues, BF16, bn, bias_block0=OFF_GATES // bn, cast_w=cast_w,
                        name=name)


def _mixer_kernel(x_ref, u_ref, v_ref, xb_ref, q_ref, ga_ref, gb_ref, gc_ref, pre_ref,
                  mk_ref, mv_ref, gv_ref, ws_ref, bs_ref, ps_ref, wpool_ref,
                  wpa_ref, wpb_ref, wpc_ref, wo_ref,
                  o_ref, sgu_scr, pool_scr, att_scr,
                  *, n_seq, seq_rows, chunk_len, pool_chunk, pos0, tiles_per_seq, zero_first_prefix):
    i = pl.program_id(0)
    tile_in_seq = i % tiles_per_seq
    tile_rows = n_seq * seq_rows

    ri = lax.broadcasted_iota(jnp.int32, (chunk_len, chunk_len), 0)
    cj = lax.broadcasted_iota(jnp.int32, (chunk_len, chunk_len), 1)
    causal = jnp.right_shift(cj, CHUNK_SHIFT) <= jnp.right_shift(ri, CHUNK_SHIFT)
    ws = [jnp.where(causal, ws_ref[g], 0.0).astype(BF16) for g in range(SGU_GROUPS)]
    bs_full = [jnp.broadcast_to(bs_ref[:, g:g + 1], (chunk_len, SGU_GDIM)) for g in range(SGU_GROUPS)]

    bt = lax.broadcasted_iota(jnp.int32, (pool_chunk, 2 * pool_chunk), 0)
    bj = lax.broadcasted_iota(jnp.int32, (pool_chunk, 2 * pool_chunk), 1)
    back = bt + pool_chunk - bj
    bands = [jnp.where(back >= 0, jnp.where(back < w, 1.0, 0.0), 0.0).astype(BF16) for w in POOL_WINDOWS]
    trow = lax.broadcasted_iota(jnp.int32, (pool_chunk, 1), 0)

    keep_prefix = jnp.where(tile_in_seq == 0, 0.0, 1.0) if zero_first_prefix else None

    for s in range(n_seq):
        r0 = s * seq_rows

        v = v_ref[r0:r0 + seq_rows, :].astype(F32)
        vn = _rms(v, gv_ref[...]).astype(BF16)
        for c in range(seq_rows // chunk_len):
            c0 = c * chunk_len
            for g in range(SGU_GROUPS):
                lo, hi = g * SGU_GDIM, (g + 1) * SGU_GDIM
                mixed = jnp.dot(ws[g], vn[c0:c0 + chunk_len, lo:hi], preferred_element_type=F32) + bs_full[g]
                u = u_ref[r0 + c0:r0 + c0 + chunk_len, lo:hi].astype(F32)
                sgu_scr[r0 + c0:r0 + c0 + chunk_len, lo:hi] = (u * mixed).astype(BF16)

        for c in range(seq_rows // pool_chunk):
            c0 = r0 + c * pool_chunk
            pos = pos0 + tile_in_seq * tile_rows + c * pool_chunk + trow
            for g, w in enumerate(POOL_WINDOWS):
                lo, hi = g * POOL_GDIM, (g + 1) * POOL_GDIM
                cur = xb_ref[c0:c0 + pool_chunk, lo:hi]
                if c == 0:
                    prev = pre_ref[s * pool_chunk:(s + 1) * pool_chunk, lo:hi]
                    if zero_first_prefix:
                        prev = (prev.astype(F32) * keep_prefix).astype(BF16)
                else:
                    prev = xb_ref[c0 - pool_chunk:c0, lo:hi]
                window = jnp.concatenate([prev, cur], axis=0)
                wsum = jnp.dot(bands[g], window, preferred_element_type=F32)
                cnt = jnp.minimum(pos + 1, w).astype(F32)
                pooled = wsum / cnt - cur.astype(F32)
                pg = jnp.dot(pooled.astype(BF16), wpool_ref[g], preferred_element_type=F32)
                pool_scr[c0:c0 + pool_chunk, lo:hi] = (pg * ps_ref[:, lo:hi]).astype(BF16)

        kb = mk_ref[s].astype(BF16)
        vb = mv_ref[s].astype(BF16)
        for h in range(MEM_HEADS):
            lo, hi = h * MEM_HDIM, (h + 1) * MEM_HDIM
            qh = q_ref[r0:r0 + seq_rows, lo:hi]
            sc = lax.dot_general(qh, kb[:, lo:hi], (((1,), (1,)), ((), ())),
                                 preferred_element_type=F32) * (MEM_HDIM ** -0.5)
            e = jnp.exp(sc - jnp.max(sc, axis=-1, keepdims=True))
            p = (e / jnp.sum(e, axis=-1, keepdims=True)).astype(BF16)
            att_scr[r0:r0 + seq_rows, lo:hi] = jnp.dot(
                p, vb[:, lo:hi], preferred_element_type=F32).astype(BF16)

    merged = ga_ref[...].astype(F32) * jnp.dot(sgu_scr[...], wpa_ref[...], preferred_element_type=F32)
    merged += gb_ref[...].astype(F32) * jnp.dot(pool_scr[...], wpb_ref[...], preferred_element_type=F32)
    merged += gc_ref[...].astype(F32) * jnp.dot(att_scr[...], wpc_ref[...], preferred_element_type=F32)
    o_ref[...] = x_ref[...] + jnp.dot(merged.astype(BF16), wo_ref[...], preferred_element_type=F32)


def _mixer(x2, z, prefix, prefix_in_z, mem_k, mem_v, gv, ws, bs_t, pscale, wpool, wpa, wpb, wpc, wo,
           *, n_seq, seq_rows, chunk_len, pos0, tiles_per_seq, name):
    m, d = x2.shape
    tm = n_seq * seq_rows
    pool_chunk = min(SGU_CHUNK, seq_rows)
    grid = (m // tm,)
    const2 = lambda i: (0, 0)
    const3 = lambda i: (0, 0, 0)
    single = pl.Buffered(1)
    gate_block0 = OFF_GATES // d
    if prefix_in_z:
        blocks_per_tile = tm // pool_chunk
        pre_spec = pl.BlockSpec((pool_chunk, POOL_WIDTH),
                                lambda i: (jnp.maximum(i * blocks_per_tile - 1, 0), OFF_XB // POOL_WIDTH))
        mem_idx = lambda i: (i // tiles_per_seq, 0, 0)
    else:
        pre_spec = pl.BlockSpec((n_seq * pool_chunk, POOL_WIDTH), lambda i: (i, 0))
        mem_idx = lambda i: (i, 0, 0)
    in_specs = [
        pl.BlockSpec((tm, d), lambda i: (i, 0)),
        pl.BlockSpec((tm, SGU_WIDTH), lambda i: (i, 0)),
        pl.BlockSpec((tm, SGU_WIDTH), lambda i: (i, 1)),
        pl.BlockSpec((tm, POOL_WIDTH), lambda i: (i, 2)),
        pl.BlockSpec((tm, MEM_WIDTH), lambda i: (i, 3)),
        pl.BlockSpec((tm, d), lambda i: (i, gate_block0)),
        pl.BlockSpec((tm, d), lambda i: (i, gate_block0 + 1)),
        pl.BlockSpec((tm, d), lambda i: (i, gate_block0 + 2)),
        pre_spec,
        pl.BlockSpec((n_seq, N_MEM, MEM_WIDTH), mem_idx),
        pl.BlockSpec((n_seq, N_MEM, MEM_WIDTH), mem_idx),
        pl.BlockSpec((1, SGU_WIDTH), const2),
        pl.BlockSpec((SGU_GROUPS, chunk_len, chunk_len), const3),
        pl.BlockSpec((chunk_len, SGU_GROUPS), const2),
        pl.BlockSpec((1, POOL_WIDTH), const2),
        pl.BlockSpec((len(POOL_WINDOWS), POOL_GDIM, POOL_GDIM), const3, pipeline_mode=single),
        pl.BlockSpec((SGU_WIDTH, d), const2, pipeline_mode=single),
        pl.BlockSpec((POOL_WIDTH, d), const2, pipeline_mode=single),
        pl.BlockSpec((MEM_WIDTH, d), const2, pipeline_mode=single),
        pl.BlockSpec((d, d), const2, pipeline_mode=single),
    ]
    kern = functools.partial(
        _mixer_kernel, n_seq=n_seq, seq_rows=seq_rows, chunk_len=chunk_len, pool_chunk=pool_chunk,
        pos0=pos0, tiles_per_seq=tiles_per_seq, zero_first_prefix=prefix_in_z)
    return pl.pallas_call(
        kern,
        grid=grid,
        in_specs=in_specs,
        out_specs=pl.BlockSpec((tm, d), lambda i: (i, 0)),
        out_shape=jax.ShapeDtypeStruct((m, d), F32),
        scratch_shapes=[pltpu.VMEM((tm, SGU_WIDTH), BF16),
                        pltpu.VMEM((tm, POOL_WIDTH), BF16),
                        pltpu.VMEM((tm, MEM_WIDTH), BF16)],
        compiler_params=pltpu.CompilerParams(
            dimension_semantics=("arbitrary",),
            vmem_limit_bytes=VMEM_LIMIT),
        name=name,
    )(x2, z, z, z, z, z, z, z, prefix, mem_k, mem_v, gv.reshape(1, -1), ws, bs_t,
      pscale.reshape(1, -1), wpool, wpa, wpb, wpc, wo)


def _ffn_kernel(*refs, n_chunks, cast_w):
    refs = list(refs)
    h_ref = refs.pop()
    if cast_w:
        wdb_ref, wub_ref, wgb_ref = refs.pop(), refs.pop(), refs.pop()
    o_ref = refs.pop()
    x_ref, g_ref, wg_ref, wu_ref, wd_ref, gf_ref = refs
    f = pl.program_id(1)
    chunks = _row_chunks(x_ref.shape[0])

    @pl.when(f == 0)
    def _():
        for rs in chunks:
            x = x_ref[rs, :]
            h_ref[rs, :] = _rms(x, g_ref[...]).astype(BF16)
            o_ref[rs, :] = x

    if cast_w:
        wgb_ref[...] = wg_ref[...].astype(BF16)
        wub_ref[...] = wu_ref[...].astype(BF16)
        wdb_ref[...] = wd_ref[...].astype(BF16)
        wg_ref, wu_ref, wd_ref = wgb_ref, wub_ref, wdb_ref

    for rs in chunks:
        h = h_ref[rs, :]
        gate = jnp.dot(h, wg_ref[...], preferred_element_type=F32)
        up = jnp.dot(h, wu_ref[...], preferred_element_type=F32)
        act = ((gate * _sigmoid(gate)) * up).astype(BF16)
        o_ref[rs, :] += jnp.dot(act, wd_ref[...], preferred_element_type=F32)

    @pl.when(f == n_chunks - 1)
    def _():
        for rs in chunks:
            o_ref[rs, :] = _rms(o_ref[rs, :], gf_ref[...])


def _ffn(x2, g_ffn, wg, wu, wd, g_final, *, cast_w, name):
    m, d = x2.shape
    d_ff = wg.shape[1]
    tm = min(FFN_ROWS, m)
    assert not cast_w or m == tm, "the bf16 weight copies are written once per d_ff chunk"
    fc = FFN_COLS_CAST if cast_w else FFN_COLS
    n_chunks = d_ff // fc
    row = lambda i, f: (i, 0)
    vec = lambda i, f: (0, 0)
    cols = pl.BlockSpec((d, fc), lambda i, f: (0, f))
    rows = pl.BlockSpec((fc, d), lambda i, f: (f, 0))
    out_specs = [pl.BlockSpec((tm, d), row)]
    out_shape = [jax.ShapeDtypeStruct((m, d), F32)]
    if cast_w:
        out_specs += [cols, cols, rows]
        out_shape += [jax.ShapeDtypeStruct(w.shape, BF16) for w in (wg, wu, wd)]
    out = pl.pallas_call(
        functools.partial(_ffn_kernel, n_chunks=n_chunks, cast_w=cast_w),
        grid=(m // tm, n_chunks),
        in_specs=[pl.BlockSpec((tm, d), row), pl.BlockSpec((1, d), vec), cols, cols, rows,
                  pl.BlockSpec((1, d), vec)],
        out_specs=out_specs,
        out_shape=out_shape,
        scratch_shapes=[pltpu.VMEM((tm, d), BF16)],
        compiler_params=pltpu.CompilerParams(
            dimension_semantics=("arbitrary", "arbitrary"),
            vmem_limit_bytes=VMEM_LIMIT),
        name=name,
    )(x2, g_ffn.reshape(1, d), wg, wu, wd, g_final.reshape(1, d))
    return tuple(out) if cast_w else out[0]


def kernel(x_prompt, x_sample, mem_prompt, state_pool, cache_mem_k, cache_mem_v, g_mix, w_in, b_gate, g_sgu_v, w_sgu, b_sgu, w_pool, pool_scale, g_mem, w_mk, w_mv, w_pa, w_pb, w_pc, w_o, g_ffn, w_ff_gate, w_ff_up, w_ff_down, g_final):
    depth = g_mix.shape[0]
    batch, seq, d = x_prompt.shape
    dec_batch, dec_seq, _ = x_sample.shape
    assert depth == 1, "final norm is fused into the last layer's FFN call"
    assert d == IN_COLS and dec_seq >= POOL_STATE

    xp = x_prompt.reshape(batch * seq, d)
    xs = x_sample.reshape(dec_batch * dec_seq, d)
    mem2 = mem_prompt.reshape(batch * N_MEM, d)

    pool_p, pool_s, mk_p, mv_p, v_s = [], [], [], [], []
    for l in range(depth):
        wpool_b = w_pool[l].astype(BF16)
        wpa_b, wpb_b, wpc_b, wo_b = (w[l].astype(BF16) for w in (w_pa, w_pb, w_pc, w_o))
        mixer_weights = (g_sgu_v[l], w_sgu[l], b_sgu[l], pool_scale[l], wpool_b, wpa_b, wpb_b, wpc_b, wo_b)

        zs, w_in_b = _in_proj(xs, g_mix[l], w_in[l], b_gate[l], True, "in_proj_sample")
        prefix = jnp.pad(state_pool[l], ((0, 0), (dec_seq - POOL_STATE, 0), (0, 0)))
        prefix = prefix.reshape(dec_batch * dec_seq, POOL_WIDTH).astype(BF16)
        gv, ws, bs, pscale = mixer_weights[:4]
        x1s = _mixer(xs, zs, prefix, False,
                     cache_mem_k[l].reshape(dec_batch, N_MEM, MEM_WIDTH),
                     cache_mem_v[l].reshape(dec_batch, N_MEM, MEM_WIDTH),
                     gv, ws[:, :dec_seq, :dec_seq], jnp.transpose(bs[:, :dec_seq]), pscale, *mixer_weights[4:],
                     n_seq=min(MIXER_SAMPLE_SEQS, dec_batch), seq_rows=dec_seq, chunk_len=dec_seq,
                     pos0=PAST_LEN, tiles_per_seq=1, name="mixer_sample")
        xs, wg_b, wu_b, wd_b = _ffn(x1s, g_ffn[l], w_ff_gate[l], w_ff_up[l], w_ff_down[l], g_final,
                                    cast_w=True, name="ffn_sample")
        zs3 = zs.reshape(dec_batch, dec_seq, -1)
        pool_s.append(zs3[:, dec_seq - POOL_STATE:, OFF_XB:OFF_XB + POOL_WIDTH].astype(F32))
        v_s.append(zs3[:, :, SGU_WIDTH:2 * SGU_WIDTH].astype(F32))

        mk = _norm_matmul(mem2, g_mem[l], w_mk[l].astype(BF16), None, ((0, 1, "identity"),), F32,
                          MEM_WIDTH, name="mem_k")
        mv = _norm_matmul(mem2, g_mem[l], w_mv[l].astype(BF16), None, ((0, 1, "identity"),), F32,
                          MEM_WIDTH, name="mem_v")
        mk_p.append(mk.reshape(batch, N_MEM, MEM_HEADS, MEM_HDIM))
        mv_p.append(mv.reshape(batch, N_MEM, MEM_HEADS, MEM_HDIM))

        zp = _in_proj(xp, g_mix[l], w_in_b, b_gate[l], False, "in_proj_prompt")
        tm_p = min(MIXER_ROWS, seq)
        x1p = _mixer(xp, zp, zp, True, mk.reshape(batch, N_MEM, MEM_WIDTH), mv.reshape(batch, N_MEM, MEM_WIDTH),
                     gv, ws, jnp.transpose(bs), pscale, *mixer_weights[4:],
                     n_seq=1, seq_rows=tm_p, chunk_len=SGU_CHUNK, pos0=0, tiles_per_seq=seq // tm_p,
                     name="mixer_prompt")
        xp = _ffn(x1p, g_ffn[l], wg_b, wu_b, wd_b, g_final, cast_w=False, name="ffn_prompt")
        xb_p = zp.reshape(batch, seq, -1)[:, seq - POOL_STATE:, OFF_XB:OFF_XB + POOL_WIDTH]
        pool_p.append(xb_p.astype(F32))

    y_prompt = xp.reshape(batch, seq, d)
    y_sample = xs.reshape(dec_batch, dec_seq, d)
    return (y_prompt, y_sample, jnp.stack(pool_p), jnp.stack(pool_s),
            jnp.stack(mk_p), jnp.stack(mv_p), jnp.stack(v_s))
```

```python
import functools

import jax
import jax.numpy as jnp
from jax import lax
from jax.experimental import pallas as pl
from jax.experimental.pallas import tpu as pltpu

F32 = jnp.float32
BF16 = jnp.bfloat16

EPS = 1e-6
CHUNK_SHIFT = 6
SGU_CHUNK = 128
SGU_WIDTH = 1024
SGU_GROUPS = 4
SGU_GDIM = SGU_WIDTH // SGU_GROUPS
POOL_WIDTH = 1024
POOL_WINDOWS = (2, 4, 8, 16)
POOL_GDIM = POOL_WIDTH // len(POOL_WINDOWS)
POOL_STATE = max(POOL_WINDOWS) - 1
N_MEM = 256
MEM_HEADS = 4
MEM_HDIM = 256
MEM_WIDTH = MEM_HEADS * MEM_HDIM
PAST_LEN = 1024
OFF_XB = 2 * SGU_WIDTH
OFF_GATES = OFF_XB + POOL_WIDTH + MEM_WIDTH

MIB = 1024 * 1024

IN_ROWS = 1024
IN_COLS = 2048
IN_COLS_CAST = 1024
MIXER_ROWS = 512
MIXER_SAMPLE_SEQS = 4
OUT_ROWS = 1024
FFN_ROWS = 1024
FFN_COLS = 512
FFN_COLS_CAST = 256
ROW_CHUNK = 512
VMEM_LIMIT = 57 * MIB


def _rms(x, gain):
    ms = jnp.mean(x * x, axis=-1, keepdims=True)
    return (x * lax.rsqrt(ms + EPS)) * gain


def _gelu_tanh(x):
    cdf = 0.5 * (1.0 + jnp.tanh(0.7978845608028654 * (x + 0.044715 * (x * x * x))))
    return x * cdf


def _sigmoid(x):
    return 0.5 + 0.5 * jnp.tanh(0.5 * x)


def _row_chunks(rows):
    step = min(ROW_CHUNK, rows)
    return [slice(r, r + step) for r in range(0, rows, step)]


def _norm_matmul_kernel(*refs, epilogues, has_bias, cast_w):
    refs = list(refs)
    h_ref = refs.pop()
    wb_ref = refs.pop() if cast_w else None
    o_ref = refs.pop()
    b_ref = refs.pop() if has_bias else None
    x_ref, g_ref, w_ref = refs
    n = pl.program_id(1)

    def block(kind, with_norm):
        if cast_w:
            wb_ref[...] = w_ref[...].astype(BF16)
        w_src = wb_ref if cast_w else w_ref
        for rs in _row_chunks(x_ref.shape[0]):
            if with_norm:
                h_ref[rs, :] = _rms(x_ref[rs, :], g_ref[...]).astype(BF16)
            z = jnp.dot(h_ref[rs, :], w_src[...], preferred_element_type=F32)
            if kind == "gelu":
                z = _gelu_tanh(z)
            elif kind == "sigmoid_bias":
                z = _sigmoid(z + b_ref[...])
            o_ref[rs, :] = z.astype(o_ref.dtype)

    for lo, hi, kind in epilogues:
        if lo == 0:
            pl.when(n == 0)(functools.partial(block, kind, True))
            lo = 1
        if hi > lo:
            pl.when((n >= lo) & (n < hi))(functools.partial(block, kind, False))


def _norm_matmul(x2, gain, w, bias, epilogues, out_dtype, bn, bias_block0=0, cast_w=False, name="norm_matmul"):
    m, k = x2.shape
    n_cols = w.shape[1]
    bm = min(IN_ROWS, m)
    grid = (m // bm, n_cols // bn)
    assert not cast_w or grid[0] == 1, "the bf16 weight copy is written once per column block"
    in_specs = [
        pl.BlockSpec((bm, k), lambda i, j: (i, 0)),
        pl.BlockSpec((1, k), lambda i, j: (0, 0)),
        pl.BlockSpec((k, bn), lambda i, j: (0, j)),
    ]
    args = [x2, gain.reshape(1, k), w]
    if bias is not None:
        in_specs.append(pl.BlockSpec((1, bn), lambda i, j: (0, jnp.maximum(j - bias_block0, 0))))
        args.append(bias.reshape(1, -1))
    out_specs = [pl.BlockSpec((bm, bn), lambda i, j: (i, j))]
    out_shape = [jax.ShapeDtypeStruct((m, n_cols), out_dtype)]
    if cast_w:
        out_specs.append(pl.BlockSpec((k, bn), lambda i, j: (0, j)))
        out_shape.append(jax.ShapeDtypeStruct((k, n_cols), BF16))
    out = pl.pallas_call(
        functools.partial(_norm_matmul_kernel, epilogues=epilogues, has_bias=bias is not None, cast_w=cast_w),
        grid=grid,
        in_specs=in_specs,
        out_specs=out_specs,
        out_shape=out_shape,
        scratch_shapes=[pltpu.VMEM((bm, k), BF16)],
        compiler_params=pltpu.CompilerParams(
            dimension_semantics=("arbitrary", "arbitrary"),
            vmem_limit_bytes=VMEM_LIMIT),
        name=name,
    )(*args)
    return tuple(out) if cast_w else out[0]


def _in_proj(x2, gain, w, bias, cast_w, name):
    bn = IN_COLS_CAST if cast_w else IN_COLS
    epilogues = ((0, OFF_XB // bn, "gelu"), (OFF_XB // bn, OFF_GATES // bn, "identity"),
                 (OFF_GATES // bn, w.shape[1] // bn, "sigmoid_bias"))
    return _norm_matmul(x2, gain, w, bias, epilogues, BF16, bn, bias_block0=OFF_GATES // bn, cast_w=cast_w,
                        name=name)


def _mixer_kernel(u_ref, v_ref, xb_ref, q_ref, ga_ref, gb_ref, gc_ref, pre_ref,
                  mk_ref, mv_ref, gv_ref, ws_ref, bs_ref, ps_ref, wpool_ref,
                  wpa_ref, wpb_ref, wpc_ref,
                  o_ref, sgu_scr, pool_scr, att_scr,
                  *, n_seq, seq_rows, chunk_len, pool_chunk, pos0, tiles_per_seq, zero_first_prefix):
    i = pl.program_id(0)
    tile_in_seq = i % tiles_per_seq
    tile_rows = n_seq * seq_rows

    ri = lax.broadcasted_iota(jnp.int32, (chunk_len, chunk_len), 0)
    cj = lax.broadcasted_iota(jnp.int32, (chunk_len, chunk_len), 1)
    causal = jnp.right_shift(cj, CHUNK_SHIFT) <= jnp.right_shift(ri, CHUNK_SHIFT)
    ws = [jnp.where(causal, ws_ref[g], 0.0).astype(BF16) for g in range(SGU_GROUPS)]
    bs_full = [jnp.broadcast_to(bs_ref[:, g:g + 1], (chunk_len, SGU_GDIM)) for g in range(SGU_GROUPS)]

    bt = lax.broadcasted_iota(jnp.int32, (pool_chunk, 2 * pool_chunk), 0)
    bj = lax.broadcasted_iota(jnp.int32, (pool_chunk, 2 * pool_chunk), 1)
    back = bt + pool_chunk - bj
    bands = [jnp.where(back >= 0, jnp.where(back < w, 1.0, 0.0), 0.0).astype(BF16) for w in POOL_WINDOWS]
    trow = lax.broadcasted_iota(jnp.int32, (pool_chunk, 1), 0)

    keep_prefix = jnp.where(tile_in_seq == 0, 0.0, 1.0) if zero_first_prefix else None

    for s in range(n_seq):
        r0 = s * seq_rows

        v = v_ref[r0:r0 + seq_rows, :].astype(F32)
        vn = _rms(v, gv_ref[...]).astype(BF16)
        for c in range(seq_rows // chunk_len):
            c0 = c * chunk_len
            for g in range(SGU_GROUPS):
                lo, hi = g * SGU_GDIM, (g + 1) * SGU_GDIM
                mixed = jnp.dot(ws[g], vn[c0:c0 + chunk_len, lo:hi], preferred_element_type=F32) + bs_full[g]
                u = u_ref[r0 + c0:r0 + c0 + chunk_len, lo:hi].astype(F32)
                sgu_scr[r0 + c0:r0 + c0 + chunk_len, lo:hi] = (u * mixed).astype(BF16)

        for c in range(seq_rows // pool_chunk):
            c0 = r0 + c * pool_chunk
            pos = pos0 + tile_in_seq * tile_rows + c * pool_chunk + trow
            for g, w in enumerate(POOL_WINDOWS):
                lo, hi = g * POOL_GDIM, (g + 1) * POOL_GDIM
                cur = xb_ref[c0:c0 + pool_chunk, lo:hi]
                if c == 0:
                    prev = pre_ref[s * pool_chunk:(s + 1) * pool_chunk, lo:hi]
                    if zero_first_prefix:
                        prev = (prev.astype(F32) * keep_prefix).astype(BF16)
                else:
                    prev = xb_ref[c0 - pool_chunk:c0, lo:hi]
                window = jnp.concatenate([prev, cur], axis=0)
                wsum = jnp.dot(bands[g], window, preferred_element_type=F32)
                cnt = jnp.minimum(pos + 1, w).astype(F32)
                pooled = wsum / cnt - cur.astype(F32)
                pg = jnp.dot(pooled.astype(BF16), wpool_ref[g], preferred_element_type=F32)
                pool_scr[c0:c0 + pool_chunk, lo:hi] = (pg * ps_ref[:, lo:hi]).astype(BF16)

        kb = mk_ref[s].astype(BF16)
        vb = mv_ref[s].astype(BF16)
        for h in range(MEM_HEADS):
            lo, hi = h * MEM_HDIM, (h + 1) * MEM_HDIM
            qh = q_ref[r0:r0 + seq_rows, lo:hi]
            sc = lax.dot_general(qh, kb[:, lo:hi], (((1,), (1,)), ((), ())),
                                 preferred_element_type=F32) * (MEM_HDIM ** -0.5)
            e = jnp.exp(sc - jnp.max(sc, axis=-1, keepdims=True))
            p = (e / jnp.sum(e, axis=-1, keepdims=True)).astype(BF16)
            att_scr[r0:r0 + seq_rows, lo:hi] = jnp.dot(
                p, vb[:, lo:hi], preferred_element_type=F32).astype(BF16)

    merged = ga_ref[...].astype(F32) * jnp.dot(sgu_scr[...], wpa_ref[...], preferred_element_type=F32)
    merged += gb_ref[...].astype(F32) * jnp.dot(pool_scr[...], wpb_ref[...], preferred_element_type=F32)
    merged += gc_ref[...].astype(F32) * jnp.dot(att_scr[...], wpc_ref[...], preferred_element_type=F32)
    o_ref[...] = merged.astype(BF16)


def _mixer(z, prefix, prefix_in_z, mem_k, mem_v, gv, ws, bs_t, pscale, wpool, wpa, wpb, wpc,
           *, n_seq, seq_rows, chunk_len, pos0, tiles_per_seq, name):
    m = z.shape[0]
    d = wpa.shape[1]
    tm = n_seq * seq_rows
    pool_chunk = min(SGU_CHUNK, seq_rows)
    grid = (m // tm,)
    const2 = lambda i: (0, 0)
    const3 = lambda i: (0, 0, 0)
    single = pl.Buffered(1)
    gate_block0 = OFF_GATES // d
    if prefix_in_z:
        blocks_per_tile = tm // pool_chunk
        pre_spec = pl.BlockSpec((pool_chunk, POOL_WIDTH),
                                lambda i: (jnp.maximum(i * blocks_per_tile - 1, 0), OFF_XB // POOL_WIDTH))
        mem_idx = lambda i: (i // tiles_per_seq, 0, 0)
    else:
        pre_spec = pl.BlockSpec((n_seq * pool_chunk, POOL_WIDTH), lambda i: (i, 0))
        mem_idx = lambda i: (i, 0, 0)
    in_specs = [
        pl.BlockSpec((tm, SGU_WIDTH), lambda i: (i, 0)),
        pl.BlockSpec((tm, SGU_WIDTH), lambda i: (i, 1)),
        pl.BlockSpec((tm, POOL_WIDTH), lambda i: (i, 2)),
        pl.BlockSpec((tm, MEM_WIDTH), lambda i: (i, 3)),
        pl.BlockSpec((tm, d), lambda i: (i, gate_block0)),
        pl.BlockSpec((tm, d), lambda i: (i, gate_block0 + 1)),
        pl.BlockSpec((tm, d), lambda i: (i, gate_block0 + 2)),
        pre_spec,
        pl.BlockSpec((n_seq, N_MEM, MEM_WIDTH), mem_idx),
        pl.BlockSpec((n_seq, N_MEM, MEM_WIDTH), mem_idx),
        pl.BlockSpec((1, SGU_WIDTH), const2),
        pl.BlockSpec((SGU_GROUPS, chunk_len, chunk_len), const3),
        pl.BlockSpec((chunk_len, SGU_GROUPS), const2),
        pl.BlockSpec((1, POOL_WIDTH), const2),
        pl.BlockSpec((len(POOL_WINDOWS), POOL_GDIM, POOL_GDIM), const3, pipeline_mode=single),
        pl.BlockSpec((SGU_WIDTH, d), const2, pipeline_mode=single),
        pl.BlockSpec((POOL_WIDTH, d), const2, pipeline_mode=single),
        pl.BlockSpec((MEM_WIDTH, d), const2, pipeline_mode=single),
    ]
    kern = functools.partial(
        _mixer_kernel, n_seq=n_seq, seq_rows=seq_rows, chunk_len=chunk_len, pool_chunk=pool_chunk,
        pos0=pos0, tiles_per_seq=tiles_per_seq, zero_first_prefix=prefix_in_z)
    return pl.pallas_call(
        kern,
        grid=grid,
        in_specs=in_specs,
        out_specs=pl.BlockSpec((tm, d), lambda i: (i, 0)),
        out_shape=jax.ShapeDtypeStruct((m, d), BF16),
        scratch_shapes=[pltpu.VMEM((tm, SGU_WIDTH), BF16),
                        pltpu.VMEM((tm, POOL_WIDTH), BF16),
                        pltpu.VMEM((tm, MEM_WIDTH), BF16)],
        compiler_params=pltpu.CompilerParams(
            dimension_semantics=("arbitrary",),
            vmem_limit_bytes=VMEM_LIMIT),
        name=name,
    )(z, z, z, z, z, z, z, prefix, mem_k, mem_v, gv.reshape(1, -1), ws, bs_t,
      pscale.reshape(1, -1), wpool, wpa, wpb, wpc)


def _out_proj_kernel(x_ref, m_ref, w_ref, o_ref):
    for rs in _row_chunks(x_ref.shape[0]):
        o_ref[rs, :] = x_ref[rs, :] + jnp.dot(m_ref[rs, :], w_ref[...], preferred_element_type=F32)


def _out_proj(x2, merged, wo, name):
    m, d = x2.shape
    tm = min(OUT_ROWS, m)
    row = lambda i: (i, 0)
    return pl.pallas_call(
        _out_proj_kernel,
        grid=(m // tm,),
        in_specs=[pl.BlockSpec((tm, d), row), pl.BlockSpec((tm, d), row),
                  pl.BlockSpec((d, d), lambda i: (0, 0), pipeline_mode=pl.Buffered(1))],
        out_specs=pl.BlockSpec((tm, d), row),
        out_shape=jax.ShapeDtypeStruct((m, d), F32),
        compiler_params=pltpu.CompilerParams(
            dimension_semantics=("arbitrary",),
            vmem_limit_bytes=VMEM_LIMIT),
        name=name,
    )(x2, merged, wo)


def _ffn_kernel(*refs, n_chunks, cast_w):
    refs = list(refs)
    h_ref = refs.pop()
    if cast_w:
        wdb_ref, wub_ref, wgb_ref = refs.pop(), refs.pop(), refs.pop()
    o_ref = refs.pop()
    x_ref, g_ref, wg_ref, wu_ref, wd_ref, gf_ref = refs
    f = pl.program_id(1)
    chunks = _row_chunks(x_ref.shape[0])

    @pl.when(f == 0)
    def _():
        for rs in chunks:
            x = x_ref[rs, :]
            h_ref[rs, :] = _rms(x, g_ref[...]).astype(BF16)
            o_ref[rs, :] = x

    if cast_w:
        wgb_ref[...] = wg_ref[...].astype(BF16)
        wub_ref[...] = wu_ref[...].astype(BF16)
        wdb_ref[...] = wd_ref[...].astype(BF16)
        wg_ref, wu_ref, wd_ref = wgb_ref, wub_ref, wdb_ref

    for rs in chunks:
        h = h_ref[rs, :]
        gate = jnp.dot(h, wg_ref[...], preferred_element_type=F32)
        up = jnp.dot(h, wu_ref[...], preferred_element_type=F32)
        act = ((gate * _sigmoid(gate)) * up).astype(BF16)
        o_ref[rs, :] += jnp.dot(act, wd_ref[...], preferred_element_type=F32)

    @pl.when(f == n_chunks - 1)
    def _():
        for rs in chunks:
            o_ref[rs, :] = _rms(o_ref[rs, :], gf_ref[...])


def _ffn(x2, g_ffn, wg, wu, wd, g_final, *, cast_w, name):
    m, d = x2.shape
    d_ff = wg.shape[1]
    tm = min(FFN_ROWS, m)
    assert not cast_w or m == tm, "the bf16 weight copies are written once per d_ff chunk"
    fc = FFN_COLS_CAST if cast_w else FFN_COLS
    n_chunks = d_ff // fc
    row = lambda i, f: (i, 0)
    vec = lambda i, f: (0, 0)
    cols = pl.BlockSpec((d, fc), lambda i, f: (0, f))
    rows = pl.BlockSpec((fc, d), lambda i, f: (f, 0))
    out_specs = [pl.BlockSpec((tm, d), row)]
    out_shape = [jax.ShapeDtypeStruct((m, d), F32)]
    if cast_w:
        out_specs += [cols, cols, rows]
        out_shape += [jax.ShapeDtypeStruct(w.shape, BF16) for w in (wg, wu, wd)]
    out = pl.pallas_call(
        functools.partial(_ffn_kernel, n_chunks=n_chunks, cast_w=cast_w),
        grid=(m // tm, n_chunks),
        in_specs=[pl.BlockSpec((tm, d), row), pl.BlockSpec((1, d), vec), cols, cols, rows,
                  pl.BlockSpec((1, d), vec)],
        out_specs=out_specs,
        out_shape=out_shape,
        scratch_shapes=[pltpu.VMEM((tm, d), BF16)],
        compiler_params=pltpu.CompilerParams(
            dimension_semantics=("arbitrary", "arbitrary"),
            vmem_limit_bytes=VMEM_LIMIT),
        name=name,
    )(x2, g_ffn.reshape(1, d), wg, wu, wd, g_final.reshape(1, d))
    return tuple(out) if cast_w else out[0]


def kernel(x_prompt, x_sample, mem_prompt, state_pool, cache_mem_k, cache_mem_v, g_mix, w_in, b_gate, g_sgu_v, w_sgu, b_sgu, w_pool, pool_scale, g_mem, w_mk, w_mv, w_pa, w_pb, w_pc, w_o, g_ffn, w_ff_gate, w_ff_up, w_ff_down, g_final):
    depth = g_mix.shape[0]
    batch, seq, d = x_prompt.shape
    dec_batch, dec_seq, _ = x_sample.shape
    assert depth == 1, "final norm is fused into the last layer's FFN call"
    assert d == IN_COLS and dec_seq >= POOL_STATE

    xp = x_prompt.reshape(batch * seq, d)
    xs = x_sample.reshape(dec_batch * dec_seq, d)
    mem2 = mem_prompt.reshape(batch * N_MEM, d)

    pool_p, pool_s, mk_p, mv_p, v_s = [], [], [], [], []
    for l in range(depth):
        wpool_b = w_pool[l].astype(BF16)
        wpa_b, wpb_b, wpc_b, wo_b = (w[l].astype(BF16) for w in (w_pa, w_pb, w_pc, w_o))
        branch_weights = (wpool_b, wpa_b, wpb_b, wpc_b)
        gv, ws, bs, pscale = g_sgu_v[l], w_sgu[l], b_sgu[l], pool_scale[l]

        zs, w_in_b = _in_proj(xs, g_mix[l], w_in[l], b_gate[l], True, "in_proj_sample")
        prefix = jnp.pad(state_pool[l], ((0, 0), (dec_seq - POOL_STATE, 0), (0, 0)))
        prefix = prefix.reshape(dec_batch * dec_seq, POOL_WIDTH).astype(BF16)
        ms = _mixer(zs, prefix, False,
                    cache_mem_k[l].reshape(dec_batch, N_MEM, MEM_WIDTH),
                    cache_mem_v[l].reshape(dec_batch, N_MEM, MEM_WIDTH),
                    gv, ws[:, :dec_seq, :dec_seq], jnp.transpose(bs[:, :dec_seq]), pscale, *branch_weights,
                    n_seq=min(MIXER_SAMPLE_SEQS, dec_batch), seq_rows=dec_seq, chunk_len=dec_seq,
                    pos0=PAST_LEN, tiles_per_seq=1, name="mixer_sample")
        x1s = _out_proj(xs, ms, wo_b, "out_proj_sample")
        xs, wg_b, wu_b, wd_b = _ffn(x1s, g_ffn[l], w_ff_gate[l], w_ff_up[l], w_ff_down[l], g_final,
                                    cast_w=True, name="ffn_sample")
        zs3 = zs.reshape(dec_batch, dec_seq, -1)
        pool_s.append(zs3[:, dec_seq - POOL_STATE:, OFF_XB:OFF_XB + POOL_WIDTH].astype(F32))
        v_s.append(zs3[:, :, SGU_WIDTH:2 * SGU_WIDTH].astype(F32))

        mk = _norm_matmul(mem2, g_mem[l], w_mk[l].astype(BF16), None, ((0, 1, "identity"),), F32,
                          MEM_WIDTH, name="mem_k")
        mv = _norm_matmul(mem2, g_mem[l], w_mv[l].astype(BF16), None, ((0, 1, "identity"),), F32,
                          MEM_WIDTH, name="mem_v")
        mk_p.append(mk.reshape(batch, N_MEM, MEM_HEADS, MEM_HDIM))
        mv_p.append(mv.reshape(batch, N_MEM, MEM_HEADS, MEM_HDIM))

        zp = _in_proj(xp, g_mix[l], w_in_b, b_gate[l], False, "in_proj_prompt")
        tm_p = min(MIXER_ROWS, seq)
        mp = _mixer(zp, zp, True, mk.reshape(batch, N_MEM, MEM_WIDTH), mv.reshape(batch, N_MEM, MEM_WIDTH),
                    gv, ws, jnp.transpose(bs), pscale, *branch_weights,
                    n_seq=1, seq_rows=tm_p, chunk_len=SGU_CHUNK, pos0=0, tiles_per_seq=seq // tm_p,
                    name="mixer_prompt")
        x1p = _out_proj(xp, mp, wo_b, "out_proj_prompt")
        xp = _ffn(x1p, g_ffn[l], wg_b, wu_b, wd_b, g_final, cast_w=False, name="ffn_prompt")
        xb_p = zp.reshape(batch, seq, -1)[:, seq - POOL_STATE:, OFF_XB:OFF_XB + POOL_WIDTH]
        pool_p.append(xb_p.astype(F32))

    y_prompt = xp.reshape(batch, seq, d)
    y_sample = xs.reshape(dec_batch, dec_seq, d)
    return (y_prompt, y_sample, jnp.stack(pool_p), jnp.stack(pool_s),
            jnp.stack(mk_p), jnp.stack(mv_p), jnp.stack(v_s))
```

```python
import functools

import jax
import jax.numpy as jnp
from jax import lax
from jax.experimental import pallas as pl
from jax.experimental.pallas import tpu as pltpu

F32 = jnp.float32
BF16 = jnp.bfloat16

EPS = 1e-6
CHUNK_SHIFT = 6
SGU_CHUNK = 128
SGU_WIDTH = 1024
SGU_GROUPS = 4
SGU_GDIM = SGU_WIDTH // SGU_GROUPS
POOL_WIDTH = 1024
POOL_WINDOWS = (2, 4, 8, 16)
POOL_GDIM = POOL_WIDTH // len(POOL_WINDOWS)
POOL_STATE = max(POOL_WINDOWS) - 1
N_MEM = 256
MEM_HEADS = 4
MEM_HDIM = 256
MEM_WIDTH = MEM_HEADS * MEM_HDIM
PAST_LEN = 1024
OFF_XB = 2 * SGU_WIDTH
OFF_GATES = OFF_XB + POOL_WIDTH + MEM_WIDTH

MIB = 1024 * 1024

IN_ROWS = 1024
IN_COLS = 2048
IN_COLS_CAST = 1024
MIXER_ROWS = 512
MIXER_SAMPLE_SEQS = 4
OUT_ROWS = 1024
OUT_MIN_STEPS = 4
FFN_ROWS = 1024
FFN_COLS = 512
FFN_COLS_CAST = 256
ROW_CHUNK = 512
VMEM_LIMIT = 57 * MIB


def _rms(x, gain):
    ms = jnp.mean(x * x, axis=-1, keepdims=True)
    return (x * lax.rsqrt(ms + EPS)) * gain


def _gelu_tanh(x):
    cdf = 0.5 * (1.0 + jnp.tanh(0.7978845608028654 * (x + 0.044715 * (x * x * x))))
    return x * cdf


def _sigmoid(x):
    return 0.5 + 0.5 * jnp.tanh(0.5 * x)


def _row_chunks(rows):
    step = min(ROW_CHUNK, rows)
    return [slice(r, r + step) for r in range(0, rows, step)]


def _norm_matmul_kernel(*refs, epilogues, has_bias, cast_w):
    refs = list(refs)
    h_ref = refs.pop()
    wb_ref = refs.pop() if cast_w else None
    o_ref = refs.pop()
    b_ref = refs.pop() if has_bias else None
    x_ref, g_ref, w_ref = refs
    n = pl.program_id(1)

    def block(kind, with_norm):
        if cast_w:
            wb_ref[...] = w_ref[...].astype(BF16)
        w_src = wb_ref if cast_w else w_ref
        for rs in _row_chunks(x_ref.shape[0]):
            if with_norm:
                h_ref[rs, :] = _rms(x_ref[rs, :], g_ref[...]).astype(BF16)
            z = jnp.dot(h_ref[rs, :], w_src[...], preferred_element_type=F32)
            if kind == "gelu":
                z = _gelu_tanh(z)
            elif kind == "sigmoid_bias":
                z = _sigmoid(z + b_ref[...])
            o_ref[rs, :] = z.astype(o_ref.dtype)

    for lo, hi, kind in epilogues:
        if lo == 0:
            pl.when(n == 0)(functools.partial(block, kind, True))
            lo = 1
        if hi > lo:
            pl.when((n >= lo) & (n < hi))(functools.partial(block, kind, False))


def _norm_matmul(x2, gain, w, bias, epilogues, out_dtype, bn, bias_block0=0, cast_w=False, name="norm_matmul"):
    m, k = x2.shape
    n_cols = w.shape[1]
    bm = min(IN_ROWS, m)
    grid = (m // bm, n_cols // bn)
    assert not cast_w or grid[0] == 1, "the bf16 weight copy is written once per column block"
    in_specs = [
        pl.BlockSpec((bm, k), lambda i, j: (i, 0)),
        pl.BlockSpec((1, k), lambda i, j: (0, 0)),
        pl.BlockSpec((k, bn), lambda i, j: (0, j)),
    ]
    args = [x2, gain.reshape(1, k), w]
    if bias is not None:
        in_specs.append(pl.BlockSpec((1, bn), lambda i, j: (0, jnp.maximum(j - bias_block0, 0))))
        args.append(bias.reshape(1, -1))
    out_specs = [pl.BlockSpec((bm, bn), lambda i, j: (i, j))]
    out_shape = [jax.ShapeDtypeStruct((m, n_cols), out_dtype)]
    if cast_w:
        out_specs.append(pl.BlockSpec((k, bn), lambda i, j: (0, j)))
        out_shape.append(jax.ShapeDtypeStruct((k, n_cols), BF16))
    out = pl.pallas_call(
        functools.partial(_norm_matmul_kernel, epilogues=epilogues, has_bias=bias is not None, cast_w=cast_w),
        grid=grid,
        in_specs=in_specs,
        out_specs=out_specs,
        out_shape=out_shape,
        scratch_shapes=[pltpu.VMEM((bm, k), BF16)],
        compiler_params=pltpu.CompilerParams(
            dimension_semantics=("arbitrary", "arbitrary"),
            vmem_limit_bytes=VMEM_LIMIT),
        name=name,
    )(*args)
    return tuple(out) if cast_w else out[0]


def _in_proj(x2, gain, w, bias, cast_w, name):
    bn = IN_COLS_CAST if cast_w else IN_COLS
    epilogues = ((0, OFF_XB // bn, "gelu"), (OFF_XB // bn, OFF_GATES // bn, "identity"),
                 (OFF_GATES // bn, w.shape[1] // bn, "sigmoid_bias"))
    return _norm_matmul(x2, gain, w, bias, epilogues, BF16, bn, bias_block0=OFF_GATES // bn, cast_w=cast_w,
                        name=name)


def _mixer_kernel(u_ref, v_ref, xb_ref, q_ref, ga_ref, gb_ref, gc_ref, pre_ref,
                  mk_ref, mv_ref, gv_ref, ws_ref, bs_ref, ps_ref, wpool_ref,
                  wpa_ref, wpb_ref, wpc_ref,
                  o_ref, sgu_scr, pool_scr, att_scr,
                  *, n_seq, seq_rows, chunk_len, pool_chunk, pos0, tiles_per_seq, zero_first_prefix):
    i = pl.program_id(0)
    tile_in_seq = i % tiles_per_seq
    tile_rows = n_seq * seq_rows

    ri = lax.broadcasted_iota(jnp.int32, (chunk_len, chunk_len), 0)
    cj = lax.broadcasted_iota(jnp.int32, (chunk_len, chunk_len), 1)
    causal = jnp.right_shift(cj, CHUNK_SHIFT) <= jnp.right_shift(ri, CHUNK_SHIFT)
    ws = [jnp.where(causal, ws_ref[g], 0.0).astype(BF16) for g in range(SGU_GROUPS)]
    bs_full = [jnp.broadcast_to(bs_ref[:, g:g + 1], (chunk_len, SGU_GDIM)) for g in range(SGU_GROUPS)]

    bt = lax.broadcasted_iota(jnp.int32, (pool_chunk, 2 * pool_chunk), 0)
    bj = lax.broadcasted_iota(jnp.int32, (pool_chunk, 2 * pool_chunk), 1)
    back = bt + pool_chunk - bj
    bands = [jnp.where(back >= 0, jnp.where(back < w, 1.0, 0.0), 0.0).astype(BF16) for w in POOL_WINDOWS]
    trow = lax.broadcasted_iota(jnp.int32, (pool_chunk, 1), 0)

    keep_prefix = jnp.where(tile_in_seq == 0, 0.0, 1.0) if zero_first_prefix else None

    for s in range(n_seq):
        r0 = s * seq_rows

        v = v_ref[r0:r0 + seq_rows, :].astype(F32)
        vn = _rms(v, gv_ref[...]).astype(BF16)
        for c in range(seq_rows // chunk_len):
            c0 = c * chunk_len
            for g in range(SGU_GROUPS):
                lo, hi = g * SGU_GDIM, (g + 1) * SGU_GDIM
                mixed = jnp.dot(ws[g], vn[c0:c0 + chunk_len, lo:hi], preferred_element_type=F32) + bs_full[g]
                u = u_ref[r0 + c0:r0 + c0 + chunk_len, lo:hi].astype(F32)
                sgu_scr[r0 + c0:r0 + c0 + chunk_len, lo:hi] = (u * mixed).astype(BF16)

        for c in range(seq_rows // pool_chunk):
            c0 = r0 + c * pool_chunk
            pos = pos0 + tile_in_seq * tile_rows + c * pool_chunk + trow
            for g, w in enumerate(POOL_WINDOWS):
                lo, hi = g * POOL_GDIM, (g + 1) * POOL_GDIM
                cur = xb_ref[c0:c0 + pool_chunk, lo:hi]
                if c == 0:
                    prev = pre_ref[s * pool_chunk:(s + 1) * pool_chunk, lo:hi]
                    if zero_first_prefix:
                        prev = (prev.astype(F32) * keep_prefix).astype(BF16)
                else:
                    prev = xb_ref[c0 - pool_chunk:c0, lo:hi]
                window = jnp.concatenate([prev, cur], axis=0)
                wsum = jnp.dot(bands[g], window, preferred_element_type=F32)
                cnt = jnp.minimum(pos + 1, w).astype(F32)
                pooled = wsum / cnt - cur.astype(F32)
                pg = jnp.dot(pooled.astype(BF16), wpool_ref[g], preferred_element_type=F32)
                pool_scr[c0:c0 + pool_chunk, lo:hi] = (pg * ps_ref[:, lo:hi]).astype(BF16)

        kb = mk_ref[s].astype(BF16)
        vb = mv_ref[s].astype(BF16)
        for h in range(MEM_HEADS):
            lo, hi = h * MEM_HDIM, (h + 1) * MEM_HDIM
            qh = q_ref[r0:r0 + seq_rows, lo:hi]
            sc = lax.dot_general(qh, kb[:, lo:hi], (((1,), (1,)), ((), ())),
                                 preferred_element_type=F32) * (MEM_HDIM ** -0.5)
            e = jnp.exp(sc - jnp.max(sc, axis=-1, keepdims=True))
            p = (e / jnp.sum(e, axis=-1, keepdims=True)).astype(BF16)
            att_scr[r0:r0 + seq_rows, lo:hi] = jnp.dot(
                p, vb[:, lo:hi], preferred_element_type=F32).astype(BF16)

    merged = ga_ref[...].astype(F32) * jnp.dot(sgu_scr[...], wpa_ref[...], preferred_element_type=F32)
    merged += gb_ref[...].astype(F32) * jnp.dot(pool_scr[...], wpb_ref[...], preferred_element_type=F32)
    merged += gc_ref[...].astype(F32) * jnp.dot(att_scr[...], wpc_ref[...], preferred_element_type=F32)
    o_ref[...] = merged.astype(BF16)


def _mixer(z, prefix, prefix_in_z, mem_k, mem_v, gv, ws, bs_t, pscale, wpool, wpa, wpb, wpc,
           *, n_seq, seq_rows, chunk_len, pos0, tiles_per_seq, name):
    m = z.shape[0]
    d = wpa.shape[1]
    tm = n_seq * seq_rows
    pool_chunk = min(SGU_CHUNK, seq_rows)
    grid = (m // tm,)
    const2 = lambda i: (0, 0)
    const3 = lambda i: (0, 0, 0)
    single = pl.Buffered(1)
    gate_block0 = OFF_GATES // d
    if prefix_in_z:
        blocks_per_tile = tm // pool_chunk
        pre_spec = pl.BlockSpec((pool_chunk, POOL_WIDTH),
                                lambda i: (jnp.maximum(i * blocks_per_tile - 1, 0), OFF_XB // POOL_WIDTH))
        mem_idx = lambda i: (i // tiles_per_seq, 0, 0)
    else:
        pre_spec = pl.BlockSpec((n_seq * pool_chunk, POOL_WIDTH), lambda i: (i, 0))
        mem_idx = lambda i: (i, 0, 0)
    in_specs = [
        pl.BlockSpec((tm, SGU_WIDTH), lambda i: (i, 0)),
        pl.BlockSpec((tm, SGU_WIDTH), lambda i: (i, 1)),
        pl.BlockSpec((tm, POOL_WIDTH), lambda i: (i, 2)),
        pl.BlockSpec((tm, MEM_WIDTH), lambda i: (i, 3)),
        pl.BlockSpec((tm, d), lambda i: (i, gate_block0)),
        pl.BlockSpec((tm, d), lambda i: (i, gate_block0 + 1)),
        pl.BlockSpec((tm, d), lambda i: (i, gate_block0 + 2)),
        pre_spec,
        pl.BlockSpec((n_seq, N_MEM, MEM_WIDTH), mem_idx),
        pl.BlockSpec((n_seq, N_MEM, MEM_WIDTH), mem_idx),
        pl.BlockSpec((1, SGU_WIDTH), const2),
        pl.BlockSpec((SGU_GROUPS, chunk_len, chunk_len), const3),
        pl.BlockSpec((chunk_len, SGU_GROUPS), const2),
        pl.BlockSpec((1, POOL_WIDTH), const2),
        pl.BlockSpec((len(POOL_WINDOWS), POOL_GDIM, POOL_GDIM), const3, pipeline_mode=single),
        pl.BlockSpec((SGU_WIDTH, d), const2, pipeline_mode=single),
        pl.BlockSpec((POOL_WIDTH, d), const2, pipeline_mode=single),
        pl.BlockSpec((MEM_WIDTH, d), const2, pipeline_mode=single),
    ]
    kern = functools.partial(
        _mixer_kernel, n_seq=n_seq, seq_rows=seq_rows, chunk_len=chunk_len, pool_chunk=pool_chunk,
        pos0=pos0, tiles_per_seq=tiles_per_seq, zero_first_prefix=prefix_in_z)
    return pl.pallas_call(
        kern,
        grid=grid,
        in_specs=in_specs,
        out_specs=pl.BlockSpec((tm, d), lambda i: (i, 0)),
        out_shape=jax.ShapeDtypeStruct((m, d), BF16),
        scratch_shapes=[pltpu.VMEM((tm, SGU_WIDTH), BF16),
                        pltpu.VMEM((tm, POOL_WIDTH), BF16),
                        pltpu.VMEM((tm, MEM_WIDTH), BF16)],
        compiler_params=pltpu.CompilerParams(
            dimension_semantics=("arbitrary",),
            vmem_limit_bytes=VMEM_LIMIT),
        name=name,
    )(z, z, z, z, z, z, z, prefix, mem_k, mem_v, gv.reshape(1, -1), ws, bs_t,
      pscale.reshape(1, -1), wpool, wpa, wpb, wpc)


def _out_proj_kernel(x_ref, m_ref, w_ref, o_ref):
    for rs in _row_chunks(x_ref.shape[0]):
        o_ref[rs, :] = x_ref[rs, :] + jnp.dot(m_ref[rs, :], w_ref[...], preferred_element_type=F32)


def _out_proj(x2, merged, wo, name):
    m, d = x2.shape
    tm = min(OUT_ROWS, m // OUT_MIN_STEPS)
    row = lambda i: (i, 0)
    return pl.pallas_call(
        _out_proj_kernel,
        grid=(m // tm,),
        in_specs=[pl.BlockSpec((tm, d), row), pl.BlockSpec((tm, d), row),
                  pl.BlockSpec((d, d), lambda i: (0, 0), pipeline_mode=pl.Buffered(1))],
        out_specs=pl.BlockSpec((tm, d), row),
        out_shape=jax.ShapeDtypeStruct((m, d), F32),
        compiler_params=pltpu.CompilerParams(
            dimension_semantics=("arbitrary",),
            vmem_limit_bytes=VMEM_LIMIT),
        name=name,
    )(x2, merged, wo)


def _ffn_kernel(*refs, n_chunks, cast_w):
    refs = list(refs)
    h_ref = refs.pop()
    if cast_w:
        wdb_ref, wub_ref, wgb_ref = refs.pop(), refs.pop(), refs.pop()
    o_ref = refs.pop()
    x_ref, g_ref, wg_ref, wu_ref, wd_ref, gf_ref = refs
    f = pl.program_id(1)

    if cast_w:
        wgb_ref[...] = wg_ref[...].astype(BF16)
        wub_ref[...] = wu_ref[...].astype(BF16)
        wdb_ref[...] = wd_ref[...].astype(BF16)
        wg_ref, wu_ref, wd_ref = wgb_ref, wub_ref, wdb_ref

    def step(first, last):
        for rs in _row_chunks(x_ref.shape[0]):
            if first:
                h_ref[rs, :] = _rms(x_ref[rs, :], g_ref[...]).astype(BF16)
            h = h_ref[rs, :]
            gate = jnp.dot(h, wg_ref[...], preferred_element_type=F32)
            up = jnp.dot(h, wu_ref[...], preferred_element_type=F32)
            act = ((gate * _sigmoid(gate)) * up).astype(BF16)
            acc = x_ref[rs, :] if first else o_ref[rs, :]
            y = acc + jnp.dot(act, wd_ref[...], preferred_element_type=F32)
            o_ref[rs, :] = _rms(y, gf_ref[...]) if last else y

    pl.when(f == 0)(functools.partial(step, True, False))
    pl.when((f > 0) & (f < n_chunks - 1))(functools.partial(step, False, False))
    pl.when(f == n_chunks - 1)(functools.partial(step, False, True))


def _ffn(x2, g_ffn, wg, wu, wd, g_final, *, cast_w, name):
    m, d = x2.shape
    d_ff = wg.shape[1]
    tm = min(FFN_ROWS, m)
    assert not cast_w or m == tm, "the bf16 weight copies are written once per d_ff chunk"
    fc = FFN_COLS_CAST if cast_w else FFN_COLS
    n_chunks = d_ff // fc
    row = lambda i, f: (i, 0)
    vec = lambda i, f: (0, 0)
    cols = pl.BlockSpec((d, fc), lambda i, f: (0, f))
    rows = pl.BlockSpec((fc, d), lambda i, f: (f, 0))
    out_specs = [pl.BlockSpec((tm, d), row)]
    out_shape = [jax.ShapeDtypeStruct((m, d), F32)]
    if cast_w:
        out_specs += [cols, cols, rows]
        out_shape += [jax.ShapeDtypeStruct(w.shape, BF16) for w in (wg, wu, wd)]
    out = pl.pallas_call(
        functools.partial(_ffn_kernel, n_chunks=n_chunks, cast_w=cast_w),
        grid=(m // tm, n_chunks),
        in_specs=[pl.BlockSpec((tm, d), row), pl.BlockSpec((1, d), vec), cols, cols, rows,
                  pl.BlockSpec((1, d), vec)],
        out_specs=out_specs,
        out_shape=out_shape,
        scratch_shapes=[pltpu.VMEM((tm, d), BF16)],
        compiler_params=pltpu.CompilerParams(
            dimension_semantics=("arbitrary", "arbitrary"),
            vmem_limit_bytes=VMEM_LIMIT),
        name=name,
    )(x2, g_ffn.reshape(1, d), wg, wu, wd, g_final.reshape(1, d))
    return tuple(out) if cast_w else out[0]


def kernel(x_prompt, x_sample, mem_prompt, state_pool, cache_mem_k, cache_mem_v, g_mix, w_in, b_gate, g_sgu_v, w_sgu, b_sgu, w_pool, pool_scale, g_mem, w_mk, w_mv, w_pa, w_pb, w_pc, w_o, g_ffn, w_ff_gate, w_ff_up, w_ff_down, g_final):
    depth = g_mix.shape[0]
    batch, seq, d = x_prompt.shape
    dec_batch, dec_seq, _ = x_sample.shape
    assert depth == 1, "final norm is fused into the last layer's FFN call"
    assert d == IN_COLS and dec_seq >= POOL_STATE

    xp = x_prompt.reshape(batch * seq, d)
    xs = x_sample.reshape(dec_batch * dec_seq, d)
    mem2 = mem_prompt.reshape(batch * N_MEM, d)

    pool_p, pool_s, mk_p, mv_p, v_s = [], [], [], [], []
    for l in range(depth):
        wpool_b = w_pool[l].astype(BF16)
        wpa_b, wpb_b, wpc_b, wo_b = (w[l].astype(BF16) for w in (w_pa, w_pb, w_pc, w_o))
        branch_weights = (wpool_b, wpa_b, wpb_b, wpc_b)
        gv, ws, bs, pscale = g_sgu_v[l], w_sgu[l], b_sgu[l], pool_scale[l]

        zs, w_in_b = _in_proj(xs, g_mix[l], w_in[l], b_gate[l], True, "in_proj_sample")
        prefix = jnp.pad(state_pool[l], ((0, 0), (dec_seq - POOL_STATE, 0), (0, 0)))
        prefix = prefix.reshape(dec_batch * dec_seq, POOL_WIDTH).astype(BF16)
        ms = _mixer(zs, prefix, False,
                    cache_mem_k[l].reshape(dec_batch, N_MEM, MEM_WIDTH),
                    cache_mem_v[l].reshape(dec_batch, N_MEM, MEM_WIDTH),
                    gv, ws[:, :dec_seq, :dec_seq], jnp.transpose(bs[:, :dec_seq]), pscale, *branch_weights,
                    n_seq=min(MIXER_SAMPLE_SEQS, dec_batch), seq_rows=dec_seq, chunk_len=dec_seq,
                    pos0=PAST_LEN, tiles_per_seq=1, name="mixer_sample")
        x1s = _out_proj(xs, ms, wo_b, "out_proj_sample")
        xs, wg_b, wu_b, wd_b = _ffn(x1s, g_ffn[l], w_ff_gate[l], w_ff_up[l], w_ff_down[l], g_final,
                                    cast_w=True, name="ffn_sample")
        zs3 = zs.reshape(dec_batch, dec_seq, -1)
        pool_s.append(zs3[:, dec_seq - POOL_STATE:, OFF_XB:OFF_XB + POOL_WIDTH].astype(F32))
        v_s.append(zs3[:, :, SGU_WIDTH:2 * SGU_WIDTH].astype(F32))

        mk = _norm_matmul(mem2, g_mem[l], w_mk[l].astype(BF16), None, ((0, 1, "identity"),), F32,
                          MEM_WIDTH, name="mem_k")
        mv = _norm_matmul(mem2, g_mem[l], w_mv[l].astype(BF16), None, ((0, 1, "identity"),), F32,
                          MEM_WIDTH, name="mem_v")
        mk_p.append(mk.reshape(batch, N_MEM, MEM_HEADS, MEM_HDIM))
        mv_p.append(mv.reshape(batch, N_MEM, MEM_HEADS, MEM_HDIM))

        zp = _in_proj(xp, g_mix[l], w_in_b, b_gate[l], False, "in_proj_prompt")
        tm_p = min(MIXER_ROWS, seq)
        mp = _mixer(zp, zp, True, mk.reshape(batch, N_MEM, MEM_WIDTH), mv.reshape(batch, N_MEM, MEM_WIDTH),
                    gv, ws, jnp.transpose(bs), pscale, *branch_weights,
                    n_seq=1, seq_rows=tm_p, chunk_len=SGU_CHUNK, pos0=0, tiles_per_seq=seq // tm_p,
                    name="mixer_prompt")
        x1p = _out_proj(xp, mp, wo_b, "out_proj_prompt")
        xp = _ffn(x1p, g_ffn[l], wg_b, wu_b, wd_b, g_final, cast_w=False, name="ffn_prompt")
        xb_p = zp.reshape(batch, seq, -1)[:, seq - POOL_STATE:, OFF_XB:OFF_XB + POOL_WIDTH]
        pool_p.append(xb_p.astype(F32))

    y_prompt = xp.reshape(batch, seq, d)
    y_sample = xs.reshape(dec_batch, dec_seq, d)
    return (y_prompt, y_sample, jnp.stack(pool_p), jnp.stack(pool_s),
            jnp.stack(mk_p), jnp.stack(mv_p), jnp.stack(v_s))
```

```python
import functools

import jax
import jax.numpy as jnp
from jax import lax
from jax.experimental import pallas as pl
from jax.experimental.pallas import tpu as pltpu

F32 = jnp.float32
BF16 = jnp.bfloat16

EPS = 1e-6
CHUNK_SHIFT = 6
SGU_CHUNK = 128
SGU_WIDTH = 1024
SGU_GROUPS = 4
SGU_GDIM = SGU_WIDTH // SGU_GROUPS
POOL_WIDTH = 1024
POOL_WINDOWS = (2, 4, 8, 16)
POOL_GDIM = POOL_WIDTH // len(POOL_WINDOWS)
POOL_STATE = max(POOL_WINDOWS) - 1
N_MEM = 256
MEM_HEADS = 4
MEM_HDIM = 256
MEM_WIDTH = MEM_HEADS * MEM_HDIM
PAST_LEN = 1024
OFF_XB = 2 * SGU_WIDTH
OFF_GATES = OFF_XB + POOL_WIDTH + MEM_WIDTH

MIB = 1024 * 1024
BF16_ROWS = 16

IN_ROWS = 1024
IN_COLS = 2048
IN_COLS_CAST = 1024
MIXER_ROWS = 512
MIXER_SAMPLE_SEQS = 4
OUT_ROWS = 1024
OUT_MIN_STEPS = 4
FFN_ROWS = 1024
FFN_COLS = 512
FFN_COLS_CAST = 256
ROW_CHUNK = 512
VMEM_LIMIT = 57 * MIB


def _rms(x, gain):
    ms = jnp.mean(x * x, axis=-1, keepdims=True)
    return (x * lax.rsqrt(ms + EPS)) * gain


def _gelu_tanh(x):
    cdf = 0.5 * (1.0 + jnp.tanh(0.7978845608028654 * (x + 0.044715 * (x * x * x))))
    return x * cdf


def _sigmoid(x):
    return 0.5 + 0.5 * jnp.tanh(0.5 * x)


def _row_chunks(rows):
    step = min(ROW_CHUNK, rows)
    return [slice(r, r + step) for r in range(0, rows, step)]


def _norm_matmul_kernel(*refs, epilogues, has_bias, cast_w, n_side):
    refs = list(refs)
    h_ref = refs.pop()
    side_out = [refs.pop() for _ in range(n_side)][::-1]
    wb_ref = refs.pop() if cast_w else None
    o_ref = refs.pop()
    side_in = [refs.pop() for _ in range(n_side)][::-1]
    b_ref = refs.pop() if has_bias else None
    x_ref, g_ref, w_ref = refs
    n = pl.program_id(1)

    def block(kind, with_norm):
        if cast_w:
            wb_ref[...] = w_ref[...].astype(BF16)
        w_src = wb_ref if cast_w else w_ref
        if with_norm:
            for src, dst in zip(side_in, side_out):
                dst[...] = src[...].astype(BF16)
        for rs in _row_chunks(x_ref.shape[0]):
            if with_norm:
                h_ref[rs, :] = _rms(x_ref[rs, :], g_ref[...]).astype(BF16)
            z = jnp.dot(h_ref[rs, :], w_src[...], preferred_element_type=F32)
            if kind == "gelu":
                z = _gelu_tanh(z)
            elif kind == "sigmoid_bias":
                z = _sigmoid(z + b_ref[...])
            o_ref[rs, :] = z.astype(o_ref.dtype)

    for lo, hi, kind in epilogues:
        if lo == 0:
            pl.when(n == 0)(functools.partial(block, kind, True))
            lo = 1
        if hi > lo:
            pl.when((n >= lo) & (n < hi))(functools.partial(block, kind, False))


def _norm_matmul(x2, gain, w, bias, epilogues, out_dtype, bn, bias_block0=0, cast_w=False, side_casts=(),
                 name="norm_matmul"):
    m, k = x2.shape
    n_cols = w.shape[1]
    bm = min(IN_ROWS, m)
    grid = (m // bm, n_cols // bn)
    assert not cast_w or grid[0] == 1, "the bf16 weight copy is written once per column block"
    assert all(s.shape[0] % (grid[0] * BF16_ROWS) == 0 for s in side_casts)
    in_specs = [
        pl.BlockSpec((bm, k), lambda i, j: (i, 0)),
        pl.BlockSpec((1, k), lambda i, j: (0, 0)),
        pl.BlockSpec((k, bn), lambda i, j: (0, j)),
    ]
    args = [x2, gain.reshape(1, k), w]
    if bias is not None:
        in_specs.append(pl.BlockSpec((1, bn), lambda i, j: (0, jnp.maximum(j - bias_block0, 0))))
        args.append(bias.reshape(1, -1))
    side_specs = [pl.BlockSpec((s.shape[0] // grid[0], s.shape[1]), lambda i, j: (i, 0)) for s in side_casts]
    in_specs += side_specs
    args += list(side_casts)
    out_specs = [pl.BlockSpec((bm, bn), lambda i, j: (i, j))]
    out_shape = [jax.ShapeDtypeStruct((m, n_cols), out_dtype)]
    if cast_w:
        out_specs.append(pl.BlockSpec((k, bn), lambda i, j: (0, j)))
        out_shape.append(jax.ShapeDtypeStruct((k, n_cols), BF16))
    out_specs += side_specs
    out_shape += [jax.ShapeDtypeStruct(s.shape, BF16) for s in side_casts]
    out = pl.pallas_call(
        functools.partial(_norm_matmul_kernel, epilogues=epilogues, has_bias=bias is not None, cast_w=cast_w,
                          n_side=len(side_casts)),
        grid=grid,
        in_specs=in_specs,
        out_specs=out_specs,
        out_shape=out_shape,
        scratch_shapes=[pltpu.VMEM((bm, k), BF16)],
        compiler_params=pltpu.CompilerParams(
            dimension_semantics=("arbitrary", "arbitrary"),
            vmem_limit_bytes=VMEM_LIMIT),
        name=name,
    )(*args)
    return tuple(out) if len(out) > 1 else out[0]


def _in_proj(x2, gain, w, bias, cast_w, name, side_casts=()):
    bn = IN_COLS_CAST if cast_w else IN_COLS
    epilogues = ((0, OFF_XB // bn, "gelu"), (OFF_XB // bn, OFF_GATES // bn, "identity"),
                 (OFF_GATES // bn, w.shape[1] // bn, "sigmoid_bias"))
    return _norm_matmul(x2, gain, w, bias, epilogues, BF16, bn, bias_block0=OFF_GATES // bn, cast_w=cast_w,
                        side_casts=side_casts, name=name)


def _mixer_kernel(u_ref, v_ref, xb_ref, q_ref, ga_ref, gb_ref, gc_ref, pre_ref,
                  mk_ref, mv_ref, gv_ref, ws_ref, bs_ref, ps_ref, wpool_ref,
                  wpa_ref, wpb_ref, wpc_ref,
                  o_ref, sgu_scr, pool_scr, att_scr,
                  *, n_seq, seq_rows, chunk_len, pool_chunk, pos0, tiles_per_seq, zero_first_prefix):
    i = pl.program_id(0)
    tile_in_seq = i % tiles_per_seq
    tile_rows = n_seq * seq_rows

    ri = lax.broadcasted_iota(jnp.int32, (chunk_len, chunk_len), 0)
    cj = lax.broadcasted_iota(jnp.int32, (chunk_len, chunk_len), 1)
    causal = jnp.right_shift(cj, CHUNK_SHIFT) <= jnp.right_shift(ri, CHUNK_SHIFT)
    ws = [jnp.where(causal, ws_ref[g], 0.0).astype(BF16) for g in range(SGU_GROUPS)]
    bs_full = [jnp.broadcast_to(bs_ref[:, g:g + 1], (chunk_len, SGU_GDIM)) for g in range(SGU_GROUPS)]

    bt = lax.broadcasted_iota(jnp.int32, (pool_chunk, 2 * pool_chunk), 0)
    bj = lax.broadcasted_iota(jnp.int32, (pool_chunk, 2 * pool_chunk), 1)
    back = bt + pool_chunk - bj
    bands = [jnp.where(back >= 0, jnp.where(back < w, 1.0, 0.0), 0.0).astype(BF16) for w in POOL_WINDOWS]
    trow = lax.broadcasted_iota(jnp.int32, (pool_chunk, 1), 0)

    keep_prefix = jnp.where(tile_in_seq == 0, 0.0, 1.0) if zero_first_prefix else None

    for s in range(n_seq):
        r0 = s * seq_rows

        v = v_ref[r0:r0 + seq_rows, :].astype(F32)
        vn = _rms(v, gv_ref[...]).astype(BF16)
        for c in range(seq_rows // chunk_len):
            c0 = c * chunk_len
            for g in range(SGU_GROUPS):
                lo, hi = g * SGU_GDIM, (g + 1) * SGU_GDIM
                mixed = jnp.dot(ws[g], vn[c0:c0 + chunk_len, lo:hi], preferred_element_type=F32) + bs_full[g]
                u = u_ref[r0 + c0:r0 + c0 + chunk_len, lo:hi].astype(F32)
                sgu_scr[r0 + c0:r0 + c0 + chunk_len, lo:hi] = (u * mixed).astype(BF16)

        for c in range(seq_rows // pool_chunk):
            c0 = r0 + c * pool_chunk
            pos = pos0 + tile_in_seq * tile_rows + c * pool_chunk + trow
            for g, w in enumerate(POOL_WINDOWS):
                lo, hi = g * POOL_GDIM, (g + 1) * POOL_GDIM
                cur = xb_ref[c0:c0 + pool_chunk, lo:hi]
                if c == 0:
                    prev = pre_ref[s * pool_chunk:(s + 1) * pool_chunk, lo:hi]
                    if zero_first_prefix:
                        prev = (prev.astype(F32) * keep_prefix).astype(BF16)
                else:
                    prev = xb_ref[c0 - pool_chunk:c0, lo:hi]
                window = jnp.concatenate([prev, cur], axis=0)
                wsum = jnp.dot(bands[g], window, preferred_element_type=F32)
                cnt = jnp.minimum(pos + 1, w).astype(F32)
                pooled = wsum / cnt - cur.astype(F32)
                pg = jnp.dot(pooled.astype(BF16), wpool_ref[g], preferred_element_type=F32)
                pool_scr[c0:c0 + pool_chunk, lo:hi] = (pg * ps_ref[:, lo:hi]).astype(BF16)

        kb = mk_ref[s].astype(BF16)
        vb = mv_ref[s].astype(BF16)
        for h in range(MEM_HEADS):
            lo, hi = h * MEM_HDIM, (h + 1) * MEM_HDIM
            qh = q_ref[r0:r0 + seq_rows, lo:hi]
            sc = lax.dot_general(qh, kb[:, lo:hi], (((1,), (1,)), ((), ())),
                                 preferred_element_type=F32) * (MEM_HDIM ** -0.5)
            e = jnp.exp(sc - jnp.max(sc, axis=-1, keepdims=True))
            p = (e / jnp.sum(e, axis=-1, keepdims=True)).astype(BF16)
            att_scr[r0:r0 + seq_rows, lo:hi] = jnp.dot(
                p, vb[:, lo:hi], preferred_element_type=F32).astype(BF16)

    merged = ga_ref[...].astype(F32) * jnp.dot(sgu_scr[...], wpa_ref[...], preferred_element_type=F32)
    merged += gb_ref[...].astype(F32) * jnp.dot(pool_scr[...], wpb_ref[...], preferred_element_type=F32)
    merged += gc_ref[...].astype(F32) * jnp.dot(att_scr[...], wpc_ref[...], preferred_element_type=F32)
    o_ref[...] = merged.astype(BF16)


def _mixer(z, prefix, prefix_in_z, mem_k, mem_v, gv, ws, bs_t, pscale, wpool, wpa, wpb, wpc,
           *, n_seq, seq_rows, chunk_len, pos0, tiles_per_seq, name):
    m = z.shape[0]
    d = wpa.shape[1]
    tm = n_seq * seq_rows
    pool_chunk = min(SGU_CHUNK, seq_rows)
    grid = (m // tm,)
    const2 = lambda i: (0, 0)
    const3 = lambda i: (0, 0, 0)
    single = pl.Buffered(1)
    gate_block0 = OFF_GATES // d
    if prefix_in_z:
        blocks_per_tile = tm // pool_chunk
        pre_spec = pl.BlockSpec((pool_chunk, POOL_WIDTH),
                                lambda i: (jnp.maximum(i * blocks_per_tile - 1, 0), OFF_XB // POOL_WIDTH))
        mem_idx = lambda i: (i // tiles_per_seq, 0, 0)
    else:
        pre_spec = pl.BlockSpec((n_seq * pool_chunk, POOL_WIDTH), lambda i: (i, 0))
        mem_idx = lambda i: (i, 0, 0)
    in_specs = [
        pl.BlockSpec((tm, SGU_WIDTH), lambda i: (i, 0)),
        pl.BlockSpec((tm, SGU_WIDTH), lambda i: (i, 1)),
        pl.BlockSpec((tm, POOL_WIDTH), lambda i: (i, 2)),
        pl.BlockSpec((tm, MEM_WIDTH), lambda i: (i, 3)),
        pl.BlockSpec((tm, d), lambda i: (i, gate_block0)),
        pl.BlockSpec((tm, d), lambda i: (i, gate_block0 + 1)),
        pl.BlockSpec((tm, d), lambda i: (i, gate_block0 + 2)),
        pre_spec,
        pl.BlockSpec((n_seq, N_MEM, MEM_WIDTH), mem_idx),
        pl.BlockSpec((n_seq, N_MEM, MEM_WIDTH), mem_idx),
        pl.BlockSpec((1, SGU_WIDTH), const2),
        pl.BlockSpec((SGU_GROUPS, chunk_len, chunk_len), const3),
        pl.BlockSpec((chunk_len, SGU_GROUPS), const2),
        pl.BlockSpec((1, POOL_WIDTH), const2),
        pl.BlockSpec((len(POOL_WINDOWS), POOL_GDIM, POOL_GDIM), const3, pipeline_mode=single),
        pl.BlockSpec((SGU_WIDTH, d), const2, pipeline_mode=single),
        pl.BlockSpec((POOL_WIDTH, d), const2, pipeline_mode=single),
        pl.BlockSpec((MEM_WIDTH, d), const2, pipeline_mode=single),
    ]
    kern = functools.partial(
        _mixer_kernel, n_seq=n_seq, seq_rows=seq_rows, chunk_len=chunk_len, pool_chunk=pool_chunk,
        pos0=pos0, tiles_per_seq=tiles_per_seq, zero_first_prefix=prefix_in_z)
    return pl.pallas_call(
        kern,
        grid=grid,
        in_specs=in_specs,
        out_specs=pl.BlockSpec((tm, d), lambda i: (i, 0)),
        out_shape=jax.ShapeDtypeStruct((m, d), BF16),
        scratch_shapes=[pltpu.VMEM((tm, SGU_WIDTH), BF16),
                        pltpu.VMEM((tm, POOL_WIDTH), BF16),
                        pltpu.VMEM((tm, MEM_WIDTH), BF16)],
        compiler_params=pltpu.CompilerParams(
            dimension_semantics=("arbitrary",),
            vmem_limit_bytes=VMEM_LIMIT),
        name=name,
    )(z, z, z, z, z, z, z, prefix, mem_k, mem_v, gv.reshape(1, -1), ws, bs_t,
      pscale.reshape(1, -1), wpool, wpa, wpb, wpc)


def _out_proj_kernel(x_ref, m_ref, w_ref, o_ref):
    for rs in _row_chunks(x_ref.shape[0]):
        o_ref[rs, :] = x_ref[rs, :] + jnp.dot(m_ref[rs, :], w_ref[...], preferred_element_type=F32)


def _out_proj(x2, merged, wo, name):
    m, d = x2.shape
    tm = min(OUT_ROWS, m // OUT_MIN_STEPS)
    row = lambda i: (i, 0)
    return pl.pallas_call(
        _out_proj_kernel,
        grid=(m // tm,),
        in_specs=[pl.BlockSpec((tm, d), row), pl.BlockSpec((tm, d), row),
                  pl.BlockSpec((d, d), lambda i: (0, 0), pipeline_mode=pl.Buffered(1))],
        out_specs=pl.BlockSpec((tm, d), row),
        out_shape=jax.ShapeDtypeStruct((m, d), F32),
        compiler_params=pltpu.CompilerParams(
            dimension_semantics=("arbitrary",),
            vmem_limit_bytes=VMEM_LIMIT),
        name=name,
    )(x2, merged, wo)


def _ffn_kernel(*refs, n_chunks, cast_w):
    refs = list(refs)
    h_ref = refs.pop()
    if cast_w:
        wdb_ref, wub_ref, wgb_ref = refs.pop(), refs.pop(), refs.pop()
    o_ref = refs.pop()
    x_ref, g_ref, wg_ref, wu_ref, wd_ref, gf_ref = refs
    f = pl.program_id(1)

    if cast_w:
        wgb_ref[...] = wg_ref[...].astype(BF16)
        wub_ref[...] = wu_ref[...].astype(BF16)
        wdb_ref[...] = wd_ref[...].astype(BF16)
        wg_ref, wu_ref, wd_ref = wgb_ref, wub_ref, wdb_ref

    def step(first, last):
        for rs in _row_chunks(x_ref.shape[0]):
            if first:
                h_ref[rs, :] = _rms(x_ref[rs, :], g_ref[...]).astype(BF16)
            h = h_ref[rs, :]
            gate = jnp.dot(h, wg_ref[...], preferred_element_type=F32)
            up = jnp.dot(h, wu_ref[...], preferred_element_type=F32)
            act = ((gate * _sigmoid(gate)) * up).astype(BF16)
            acc = x_ref[rs, :] if first else o_ref[rs, :]
            y = acc + jnp.dot(act, wd_ref[...], preferred_element_type=F32)
            o_ref[rs, :] = _rms(y, gf_ref[...]) if last else y

    pl.when(f == 0)(functools.partial(step, True, False))
    pl.when((f > 0) & (f < n_chunks - 1))(functools.partial(step, False, False))
    pl.when(f == n_chunks - 1)(functools.partial(step, False, True))


def _ffn(x2, g_ffn, wg, wu, wd, g_final, *, cast_w, name):
    m, d = x2.shape
    d_ff = wg.shape[1]
    tm = min(FFN_ROWS, m)
    assert not cast_w or m == tm, "the bf16 weight copies are written once per d_ff chunk"
    fc = FFN_COLS_CAST if cast_w else FFN_COLS
    n_chunks = d_ff // fc
    row = lambda i, f: (i, 0)
    vec = lambda i, f: (0, 0)
    cols = pl.BlockSpec((d, fc), lambda i, f: (0, f))
    rows = pl.BlockSpec((fc, d), lambda i, f: (f, 0))
    out_specs = [pl.BlockSpec((tm, d), row)]
    out_shape = [jax.ShapeDtypeStruct((m, d), F32)]
    if cast_w:
        out_specs += [cols, cols, rows]
        out_shape += [jax.ShapeDtypeStruct(w.shape, BF16) for w in (wg, wu, wd)]
    out = pl.pallas_call(
        functools.partial(_ffn_kernel, n_chunks=n_chunks, cast_w=cast_w),
        grid=(m // tm, n_chunks),
        in_specs=[pl.BlockSpec((tm, d), row), pl.BlockSpec((1, d), vec), cols, cols, rows,
                  pl.BlockSpec((1, d), vec)],
        out_specs=out_specs,
        out_shape=out_shape,
        scratch_shapes=[pltpu.VMEM((tm, d), BF16)],
        compiler_params=pltpu.CompilerParams(
            dimension_semantics=("arbitrary", "arbitrary"),
            vmem_limit_bytes=VMEM_LIMIT),
        name=name,
    )(x2, g_ffn.reshape(1, d), wg, wu, wd, g_final.reshape(1, d))
    return tuple(out) if cast_w else out[0]


def kernel(x_prompt, x_sample, mem_prompt, state_pool, cache_mem_k, cache_mem_v, g_mix, w_in, b_gate, g_sgu_v, w_sgu, b_sgu, w_pool, pool_scale, g_mem, w_mk, w_mv, w_pa, w_pb, w_pc, w_o, g_ffn, w_ff_gate, w_ff_up, w_ff_down, g_final):
    depth = g_mix.shape[0]
    batch, seq, d = x_prompt.shape
    dec_batch, dec_seq, _ = x_sample.shape
    assert depth == 1, "final norm is fused into the last layer's FFN call"
    assert d == IN_COLS and dec_seq >= POOL_STATE

    xp = x_prompt.reshape(batch * seq, d)
    xs = x_sample.reshape(dec_batch * dec_seq, d)
    mem2 = mem_prompt.reshape(batch * N_MEM, d)

    pool_p, pool_s, mk_p, mv_p, v_s = [], [], [], [], []
    for l in range(depth):
        gv, ws, bs, pscale = g_sgu_v[l], w_sgu[l], b_sgu[l], pool_scale[l]

        zs, w_in_b = _in_proj(xs, g_mix[l], w_in[l], b_gate[l], True, "in_proj_sample")
        later_weights = (w_pa[l], w_pb[l], w_pc[l], w_o[l], w_pool[l].reshape(-1, POOL_GDIM), w_mk[l], w_mv[l])
        zp, wpa_b, wpb_b, wpc_b, wo_b, wpool_b, wmk_b, wmv_b = _in_proj(
            xp, g_mix[l], w_in_b, b_gate[l], False, "in_proj_prompt", side_casts=later_weights)
        branch_weights = (wpool_b.reshape(w_pool[l].shape), wpa_b, wpb_b, wpc_b)

        prefix = jnp.pad(state_pool[l], ((0, 0), (dec_seq - POOL_STATE, 0), (0, 0)))
        prefix = prefix.reshape(dec_batch * dec_seq, POOL_WIDTH).astype(BF16)
        ms = _mixer(zs, prefix, False,
                    cache_mem_k[l].reshape(dec_batch, N_MEM, MEM_WIDTH).astype(BF16),
                    cache_mem_v[l].reshape(dec_batch, N_MEM, MEM_WIDTH).astype(BF16),
                    gv, ws[:, :dec_seq, :dec_seq], jnp.transpose(bs[:, :dec_seq]), pscale, *branch_weights,
                    n_seq=min(MIXER_SAMPLE_SEQS, dec_batch), seq_rows=dec_seq, chunk_len=dec_seq,
                    pos0=PAST_LEN, tiles_per_seq=1, name="mixer_sample")
        x1s = _out_proj(xs, ms, wo_b, "out_proj_sample")
        xs, wg_b, wu_b, wd_b = _ffn(x1s, g_ffn[l], w_ff_gate[l], w_ff_up[l], w_ff_down[l], g_final,
                                    cast_w=True, name="ffn_sample")
        zs3 = zs.reshape(dec_batch, dec_seq, -1)
        pool_s.append(zs3[:, dec_seq - POOL_STATE:, OFF_XB:OFF_XB + POOL_WIDTH].astype(F32))
        v_s.append(zs3[:, :, SGU_WIDTH:2 * SGU_WIDTH].astype(F32))

        mk = _norm_matmul(mem2, g_mem[l], wmk_b, None, ((0, 1, "identity"),), F32, MEM_WIDTH, name="mem_k")
        mv = _norm_matmul(mem2, g_mem[l], wmv_b, None, ((0, 1, "identity"),), F32, MEM_WIDTH, name="mem_v")
        mk_p.append(mk.reshape(batch, N_MEM, MEM_HEADS, MEM_HDIM))
        mv_p.append(mv.reshape(batch, N_MEM, MEM_HEADS, MEM_HDIM))

        tm_p = min(MIXER_ROWS, seq)
        mp = _mixer(zp, zp, True, mk.reshape(batch, N_MEM, MEM_WIDTH), mv.reshape(batch, N_MEM, MEM_WIDTH),
                    gv, ws, jnp.transpose(bs), pscale, *branch_weights,
                    n_seq=1, seq_rows=tm_p, chunk_len=SGU_CHUNK, pos0=0, tiles_per_seq=seq // tm_p,
                    name="mixer_prompt")
        x1p = _out_proj(xp, mp, wo_b, "out_proj_prompt")
        xp = _ffn(x1p, g_ffn[l], wg_b, wu_b, wd_b, g_final, cast_w=False, name="ffn_prompt")
        xb_p = zp.reshape(batch, seq, -1)[:, seq - POOL_STATE:, OFF_XB:OFF_XB + POOL_WIDTH]
        pool_p.append(xb_p.astype(F32))

    y_prompt = xp.reshape(batch, seq, d)
    y_sample = xs.reshape(dec_batch, dec_seq, d)
    return (y_prompt, y_sample, jnp.stack(pool_p), jnp.stack(pool_s),
            jnp.stack(mk_p), jnp.stack(mv_p), jnp.stack(v_s))
```

```python
import functools

import jax
import jax.numpy as jnp
from jax import lax
from jax.experimental import pallas as pl
from jax.experimental.pallas import tpu as pltpu

F32 = jnp.float32
BF16 = jnp.bfloat16

EPS = 1e-6
CHUNK_SHIFT = 6
SGU_CHUNK = 128
SGU_WIDTH = 1024
SGU_GROUPS = 4
SGU_GDIM = SGU_WIDTH // SGU_GROUPS
POOL_WIDTH = 1024
POOL_WINDOWS = (2, 4, 8, 16)
POOL_GDIM = POOL_WIDTH // len(POOL_WINDOWS)
POOL_STATE = max(POOL_WINDOWS) - 1
N_MEM = 256
MEM_HEADS = 4
MEM_HDIM = 256
MEM_WIDTH = MEM_HEADS * MEM_HDIM
PAST_LEN = 1024
OFF_V = SGU_WIDTH
OFF_XB = 2 * SGU_WIDTH
OFF_Q = OFF_XB + POOL_WIDTH
OFF_GATES = OFF_Q + MEM_WIDTH

MIB = 1024 * 1024

IN_ROWS = 1024
IN_COLS = 2048
IN_COLS_CAST = 1024
MIXER_ROWS = 512
MIXER_SAMPLE_SEQS = 4
OUT_ROWS = 1024
OUT_MIN_STEPS = 4
FFN_ROWS = 1024
FFN_COLS = 512
FFN_COLS_CAST = 256
ROW_CHUNK = 512
VMEM_LIMIT = 57 * MIB


def _rms(x, gain):
    ms = jnp.mean(x * x, axis=-1, keepdims=True)
    return (x * lax.rsqrt(ms + EPS)) * gain


def _gelu_tanh(x):
    cdf = 0.5 * (1.0 + jnp.tanh(0.7978845608028654 * (x + 0.044715 * (x * x * x))))
    return x * cdf


def _sigmoid(x):
    return 0.5 + 0.5 * jnp.tanh(0.5 * x)


def _row_chunks(rows):
    step = min(ROW_CHUNK, rows)
    return [slice(r, r + step) for r in range(0, rows, step)]


def _norm_matmul_kernel(*refs, epilogues, bias_block0, cast_w):
    refs = list(refs)
    h_ref = refs.pop()
    wb_ref = refs.pop() if cast_w else None
    o_ref = refs.pop()
    b_ref = refs.pop() if bias_block0 is not None else None
    x_ref, g_ref, w_ref = refs
    n = pl.program_id(1)

    def block(kind, with_norm):
        if cast_w:
            wb_ref[...] = w_ref[...].astype(BF16)
        w_src = wb_ref if cast_w else w_ref
        for rs in _row_chunks(x_ref.shape[0]):
            if with_norm:
                h_ref[rs, :] = _rms(x_ref[rs, :], g_ref[...]).astype(BF16)
            z = jnp.dot(h_ref[rs, :], w_src[...], preferred_element_type=F32)
            if kind == "gelu":
                z = _gelu_tanh(z)
            elif kind == "sigmoid_bias":
                z = _sigmoid(z + b_ref[n - bias_block0])
            o_ref[rs, :] = z.astype(o_ref.dtype)

    for lo, hi, kind in epilogues:
        if lo == 0:
            pl.when(n == 0)(functools.partial(block, kind, True))
            lo = 1
        if hi > lo:
            pl.when((n >= lo) & (n < hi))(functools.partial(block, kind, False))


def _norm_matmul(x2, gain, w, bias, epilogues, out_dtype, bn, bias_block0=None, cast_w=False, name="norm_matmul"):
    m, k = x2.shape
    n_cols = w.shape[1]
    bm = min(IN_ROWS, m)
    grid = (m // bm, n_cols // bn)
    assert not cast_w or grid[0] == 1, "the bf16 weight copy is written once per column block"
    in_specs = [
        pl.BlockSpec((bm, k), lambda i, j: (i, 0)),
        pl.BlockSpec((1, k), lambda i, j: (0, 0)),
        pl.BlockSpec((k, bn), lambda i, j: (0, j)),
    ]
    args = [x2, gain.reshape(1, k), w]
    if bias is not None:
        bias3 = bias.reshape(-1, 1, bn)
        in_specs.append(pl.BlockSpec(bias3.shape, lambda i, j: (0, 0, 0)))
        args.append(bias3)
    out_specs = [pl.BlockSpec((bm, bn), lambda i, j: (i, j))]
    out_shape = [jax.ShapeDtypeStruct((m, n_cols), out_dtype)]
    if cast_w:
        out_specs.append(pl.BlockSpec((k, bn), lambda i, j: (0, j)))
        out_shape.append(jax.ShapeDtypeStruct((k, n_cols), BF16))
    out = pl.pallas_call(
        functools.partial(_norm_matmul_kernel, epilogues=epilogues,
                          bias_block0=bias_block0 if bias is not None else None, cast_w=cast_w),
        grid=grid,
        in_specs=in_specs,
        out_specs=out_specs,
        out_shape=out_shape,
        scratch_shapes=[pltpu.VMEM((bm, k), BF16)],
        compiler_params=pltpu.CompilerParams(
            dimension_semantics=("arbitrary", "arbitrary"),
            vmem_limit_bytes=VMEM_LIMIT),
        name=name,
    )(*args)
    return tuple(out) if cast_w else out[0]


def _in_proj(x2, gain, w, bias, cast_w, name):
    bn = IN_COLS_CAST if cast_w else IN_COLS
    epilogues = ((0, OFF_XB // bn, "gelu"), (OFF_XB // bn, OFF_GATES // bn, "identity"),
                 (OFF_GATES // bn, w.shape[1] // bn, "sigmoid_bias"))
    return _norm_matmul(x2, gain, w, bias, epilogues, BF16, bn, bias_block0=OFF_GATES // bn, cast_w=cast_w,
                        name=name)


def _mixer_kernel(z_ref, ga_ref, gb_ref, gc_ref, pre_ref,
                  mk_ref, mv_ref, gv_ref, ws_ref, bs_ref, ps_ref, wpool_ref,
                  wpa_ref, wpb_ref, wpc_ref,
                  o_ref, sgu_scr, pool_scr, att_scr,
                  *, n_seq, seq_rows, chunk_len, pool_chunk, pos0, tiles_per_seq, zero_first_prefix):
    i = pl.program_id(0)
    tile_in_seq = i % tiles_per_seq
    tile_rows = n_seq * seq_rows

    ri = lax.broadcasted_iota(jnp.int32, (chunk_len, chunk_len), 0)
    cj = lax.broadcasted_iota(jnp.int32, (chunk_len, chunk_len), 1)
    causal = jnp.right_shift(cj, CHUNK_SHIFT) <= jnp.right_shift(ri, CHUNK_SHIFT)
    ws = [jnp.where(causal, ws_ref[g], 0.0).astype(BF16) for g in range(SGU_GROUPS)]
    bs_full = [jnp.broadcast_to(bs_ref[:, g:g + 1], (chunk_len, SGU_GDIM)) for g in range(SGU_GROUPS)]

    bt = lax.broadcasted_iota(jnp.int32, (pool_chunk, 2 * pool_chunk), 0)
    bj = lax.broadcasted_iota(jnp.int32, (pool_chunk, 2 * pool_chunk), 1)
    back = bt + pool_chunk - bj
    bands = [jnp.where(back >= 0, jnp.where(back < w, 1.0, 0.0), 0.0).astype(BF16) for w in POOL_WINDOWS]
    trow = lax.broadcasted_iota(jnp.int32, (pool_chunk, 1), 0)

    keep_prefix = jnp.where(tile_in_seq == 0, 0.0, 1.0) if zero_first_prefix else None

    for s in range(n_seq):
        r0 = s * seq_rows

        v = z_ref[r0:r0 + seq_rows, OFF_V:OFF_V + SGU_WIDTH].astype(F32)
        vn = _rms(v, gv_ref[...]).astype(BF16)
        for c in range(seq_rows // chunk_len):
            c0 = c * chunk_len
            for g in range(SGU_GROUPS):
                lo, hi = g * SGU_GDIM, (g + 1) * SGU_GDIM
                mixed = jnp.dot(ws[g], vn[c0:c0 + chunk_len, lo:hi], preferred_element_type=F32) + bs_full[g]
                u = z_ref[r0 + c0:r0 + c0 + chunk_len, lo:hi].astype(F32)
                sgu_scr[r0 + c0:r0 + c0 + chunk_len, lo:hi] = (u * mixed).astype(BF16)

        for c in range(seq_rows // pool_chunk):
            c0 = r0 + c * pool_chunk
            pos = pos0 + tile_in_seq * tile_rows + c * pool_chunk + trow
            for g, w in enumerate(POOL_WINDOWS):
                lo, hi = g * POOL_GDIM, (g + 1) * POOL_GDIM
                cur = z_ref[c0:c0 + pool_chunk, OFF_XB + lo:OFF_XB + hi]
                if c == 0:
                    prev = pre_ref[s * pool_chunk:(s + 1) * pool_chunk, lo:hi]
                    if zero_first_prefix:
                        prev = (prev.astype(F32) * keep_prefix).astype(BF16)
                else:
                    prev = z_ref[c0 - pool_chunk:c0, OFF_XB + lo:OFF_XB + hi]
                window = jnp.concatenate([prev, cur], axis=0)
                wsum = jnp.dot(bands[g], window, preferred_element_type=F32)
                cnt = jnp.minimum(pos + 1, w).astype(F32)
                pooled = wsum / cnt - cur.astype(F32)
                pg = jnp.dot(pooled.astype(BF16), wpool_ref[g], preferred_element_type=F32)
                pool_scr[c0:c0 + pool_chunk, lo:hi] = (pg * ps_ref[:, lo:hi]).astype(BF16)

        kb = mk_ref[s].astype(BF16)
        vb = mv_ref[s].astype(BF16)
        for h in range(MEM_HEADS):
            lo, hi = h * MEM_HDIM, (h + 1) * MEM_HDIM
            qh = z_ref[r0:r0 + seq_rows, OFF_Q + lo:OFF_Q + hi]
            sc = lax.dot_general(qh, kb[:, lo:hi], (((1,), (1,)), ((), ())),
                                 preferred_element_type=F32) * (MEM_HDIM ** -0.5)
            e = jnp.exp(sc - jnp.max(sc, axis=-1, keepdims=True))
            p = (e / jnp.sum(e, axis=-1, keepdims=True)).astype(BF16)
            att_scr[r0:r0 + seq_rows, lo:hi] = jnp.dot(
                p, vb[:, lo:hi], preferred_element_type=F32).astype(BF16)

    merged = ga_ref[...].astype(F32) * jnp.dot(sgu_scr[...], wpa_ref[...], preferred_element_type=F32)
    merged += gb_ref[...].astype(F32) * jnp.dot(pool_scr[...], wpb_ref[...], preferred_element_type=F32)
    merged += gc_ref[...].astype(F32) * jnp.dot(att_scr[...], wpc_ref[...], preferred_element_type=F32)
    o_ref[...] = merged.astype(BF16)


def _mixer(z, prefix, prefix_in_z, mem_k, mem_v, gv, ws, bs_t, pscale, wpool, wpa, wpb, wpc,
           *, n_seq, seq_rows, chunk_len, pos0, tiles_per_seq, name):
    m = z.shape[0]
    d = wpa.shape[1]
    tm = n_seq * seq_rows
    pool_chunk = min(SGU_CHUNK, seq_rows)
    grid = (m // tm,)
    const2 = lambda i: (0, 0)
    const3 = lambda i: (0, 0, 0)
    single = pl.Buffered(1)
    gate_block0 = OFF_GATES // d
    if prefix_in_z:
        blocks_per_tile = tm // pool_chunk
        pre_spec = pl.BlockSpec((pool_chunk, POOL_WIDTH),
                                lambda i: (jnp.maximum(i * blocks_per_tile - 1, 0), OFF_XB // POOL_WIDTH))
        mem_idx = lambda i: (i // tiles_per_seq, 0, 0)
    else:
        pre_spec = pl.BlockSpec((n_seq * pool_chunk, POOL_WIDTH), lambda i: (i, 0))
        mem_idx = lambda i: (i, 0, 0)
    in_specs = [
        pl.BlockSpec((tm, OFF_GATES), lambda i: (i, 0)),
        pl.BlockSpec((tm, d), lambda i: (i, gate_block0)),
        pl.BlockSpec((tm, d), lambda i: (i, gate_block0 + 1)),
        pl.BlockSpec((tm, d), lambda i: (i, gate_block0 + 2)),
        pre_spec,
        pl.BlockSpec((n_seq, N_MEM, MEM_WIDTH), mem_idx),
        pl.BlockSpec((n_seq, N_MEM, MEM_WIDTH), mem_idx),
        pl.BlockSpec((1, SGU_WIDTH), const2),
        pl.BlockSpec((SGU_GROUPS, chunk_len, chunk_len), const3),
        pl.BlockSpec((chunk_len, SGU_GROUPS), const2),
        pl.BlockSpec((1, POOL_WIDTH), const2),
        pl.BlockSpec((len(POOL_WINDOWS), POOL_GDIM, POOL_GDIM), const3, pipeline_mode=single),
        pl.BlockSpec((SGU_WIDTH, d), const2, pipeline_mode=single),
        pl.BlockSpec((POOL_WIDTH, d), const2, pipeline_mode=single),
        pl.BlockSpec((MEM_WIDTH, d), const2, pipeline_mode=single),
    ]
    kern = functools.partial(
        _mixer_kernel, n_seq=n_seq, seq_rows=seq_rows, chunk_len=chunk_len, pool_chunk=pool_chunk,
        pos0=pos0, tiles_per_seq=tiles_per_seq, zero_first_prefix=prefix_in_z)
    return pl.pallas_call(
        kern,
        grid=grid,
        in_specs=in_specs,
        out_specs=pl.BlockSpec((tm, d), lambda i: (i, 0)),
        out_shape=jax.ShapeDtypeStruct((m, d), BF16),
        scratch_shapes=[pltpu.VMEM((tm, SGU_WIDTH), BF16),
                        pltpu.VMEM((tm, POOL_WIDTH), BF16),
                        pltpu.VMEM((tm, MEM_WIDTH), BF16)],
        compiler_params=pltpu.CompilerParams(
            dimension_semantics=("arbitrary",),
            vmem_limit_bytes=VMEM_LIMIT),
        name=name,
    )(z, z, z, z, prefix, mem_k, mem_v, gv.reshape(1, -1), ws, bs_t,
      pscale.reshape(1, -1), wpool, wpa, wpb, wpc)


def _out_proj_kernel(x_ref, m_ref, w_ref, o_ref):
    for rs in _row_chunks(x_ref.shape[0]):
        o_ref[rs, :] = x_ref[rs, :] + jnp.dot(m_ref[rs, :], w_ref[...], preferred_element_type=F32)


def _out_proj(x2, merged, wo, name):
    m, d = x2.shape
    tm = min(OUT_ROWS, m // OUT_MIN_STEPS)
    row = lambda i: (i, 0)
    return pl.pallas_call(
        _out_proj_kernel,
        grid=(m // tm,),
        in_specs=[pl.BlockSpec((tm, d), row), pl.BlockSpec((tm, d), row),
                  pl.BlockSpec((d, d), lambda i: (0, 0), pipeline_mode=pl.Buffered(1))],
        out_specs=pl.BlockSpec((tm, d), row),
        out_shape=jax.ShapeDtypeStruct((m, d), F32),
        compiler_params=pltpu.CompilerParams(
            dimension_semantics=("arbitrary",),
            vmem_limit_bytes=VMEM_LIMIT),
        name=name,
    )(x2, merged, wo)


def _ffn_kernel(*refs, n_chunks, cast_w):
    refs = list(refs)
    h_ref = refs.pop()
    if cast_w:
        wdb_ref, wgub_ref = refs.pop(), refs.pop()
        o_ref = refs.pop()
        x_ref, g_ref, wg_ref, wu_ref, wd_ref, gf_ref = refs
        wgub_ref[:, :FFN_COLS_CAST] = wg_ref[...].astype(BF16)
        wgub_ref[:, FFN_COLS_CAST:] = wu_ref[...].astype(BF16)
        wdb_ref[...] = wd_ref[...].astype(BF16)
        wgu_ref, wd_ref = wgub_ref, wdb_ref
    else:
        o_ref = refs.pop()
        x_ref, g_ref, wgu_ref, wd_ref, gf_ref = refs
    f = pl.program_id(1)
    groups = wd_ref.shape[0] // FFN_COLS_CAST

    def step(first, last):
        for rs in _row_chunks(x_ref.shape[0]):
            if first:
                h_ref[rs, :] = _rms(x_ref[rs, :], g_ref[...]).astype(BF16)
            gu = jnp.dot(h_ref[rs, :], wgu_ref[...], preferred_element_type=F32)
            acts = []
            for p in range(groups):
                gate = gu[:, (2 * p) * FFN_COLS_CAST:(2 * p + 1) * FFN_COLS_CAST]
                up = gu[:, (2 * p + 1) * FFN_COLS_CAST:(2 * p + 2) * FFN_COLS_CAST]
                acts.append(((gate * _sigmoid(gate)) * up).astype(BF16))
            act = acts[0] if groups == 1 else jnp.concatenate(acts, axis=1)
            acc = x_ref[rs, :] if first else o_ref[rs, :]
            y = acc + jnp.dot(act, wd_ref[...], preferred_element_type=F32)
            o_ref[rs, :] = _rms(y, gf_ref[...]) if last else y

    pl.when(f == 0)(functools.partial(step, True, False))
    pl.when((f > 0) & (f < n_chunks - 1))(functools.partial(step, False, False))
    pl.when(f == n_chunks - 1)(functools.partial(step, False, True))


def _ffn(x2, g_ffn, weights, g_final, *, cast_w, name):
    m, d = x2.shape
    d_ff = weights[-1].shape[0]
    tm = min(FFN_ROWS, m)
    assert not cast_w or m == tm, "the bf16 weight copies are written once per d_ff chunk"
    fc = FFN_COLS_CAST if cast_w else FFN_COLS
    n_chunks = d_ff // fc
    row = lambda i, f: (i, 0)
    vec = lambda i, f: (0, 0)
    cols = pl.BlockSpec((d, fc), lambda i, f: (0, f))
    cols2 = pl.BlockSpec((d, 2 * fc), lambda i, f: (0, f))
    rows = pl.BlockSpec((fc, d), lambda i, f: (f, 0))
    out_specs = [pl.BlockSpec((tm, d), row)]
    out_shape = [jax.ShapeDtypeStruct((m, d), F32)]
    if cast_w:
        w_specs = [cols, cols, rows]
        out_specs += [cols2, rows]
        out_shape += [jax.ShapeDtypeStruct((d, 2 * d_ff), BF16), jax.ShapeDtypeStruct((d_ff, d), BF16)]
    else:
        w_specs = [cols2, rows]
    out = pl.pallas_call(
        functools.partial(_ffn_kernel, n_chunks=n_chunks, cast_w=cast_w),
        grid=(m // tm, n_chunks),
        in_specs=[pl.BlockSpec((tm, d), row), pl.BlockSpec((1, d), vec)] + w_specs + [pl.BlockSpec((1, d), vec)],
        out_specs=out_specs,
        out_shape=out_shape,
        scratch_shapes=[pltpu.VMEM((tm, d), BF16)],
        compiler_params=pltpu.CompilerParams(
            dimension_semantics=("arbitrary", "arbitrary"),
            vmem_limit_bytes=VMEM_LIMIT),
        name=name,
    )(x2, g_ffn.reshape(1, d), *weights, g_final.reshape(1, d))
    return tuple(out) if cast_w else out[0]


def kernel(x_prompt, x_sample, mem_prompt, state_pool, cache_mem_k, cache_mem_v, g_mix, w_in, b_gate, g_sgu_v, w_sgu, b_sgu, w_pool, pool_scale, g_mem, w_mk, w_mv, w_pa, w_pb, w_pc, w_o, g_ffn, w_ff_gate, w_ff_up, w_ff_down, g_final):
    depth = g_mix.shape[0]
    batch, seq, d = x_prompt.shape
    dec_batch, dec_seq, _ = x_sample.shape
    assert depth == 1, "final norm is fused into the last layer's FFN call"
    assert d == IN_COLS and dec_seq >= POOL_STATE

    xp = x_prompt.reshape(batch * seq, d)
    xs = x_sample.reshape(dec_batch * dec_seq, d)
    mem2 = mem_prompt.reshape(batch * N_MEM, d)

    pool_p, pool_s, mk_p, mv_p, v_s = [], [], [], [], []
    for l in range(depth):
        wpool_b = w_pool[l].astype(BF16)
        wpa_b, wpb_b, wpc_b, wo_b = (w[l].astype(BF16) for w in (w_pa, w_pb, w_pc, w_o))
        branch_weights = (wpool_b, wpa_b, wpb_b, wpc_b)
        gv, ws, bs, pscale = g_sgu_v[l], w_sgu[l], b_sgu[l], pool_scale[l]

        zs, w_in_b = _in_proj(xs, g_mix[l], w_in[l], b_gate[l], True, "in_proj_sample")
        prefix = jnp.pad(state_pool[l], ((0, 0), (dec_seq - POOL_STATE, 0), (0, 0)))
        prefix = prefix.reshape(dec_batch * dec_seq, POOL_WIDTH).astype(BF16)
        ms = _mixer(zs, prefix, False,
                    cache_mem_k[l].reshape(dec_batch, N_MEM, MEM_WIDTH),
                    cache_mem_v[l].reshape(dec_batch, N_MEM, MEM_WIDTH),
                    gv, ws[:, :dec_seq, :dec_seq], jnp.transpose(bs[:, :dec_seq]), pscale, *branch_weights,
                    n_seq=min(MIXER_SAMPLE_SEQS, dec_batch), seq_rows=dec_seq, chunk_len=dec_seq,
                    pos0=PAST_LEN, tiles_per_seq=1, name="mixer_sample")
        x1s = _out_proj(xs, ms, wo_b, "out_proj_sample")
        xs, wgu_b, wd_b = _ffn(x1s, g_ffn[l], (w_ff_gate[l], w_ff_up[l], w_ff_down[l]), g_final,
                               cast_w=True, name="ffn_sample")
        zs3 = zs.reshape(dec_batch, dec_seq, -1)
        pool_s.append(zs3[:, dec_seq - POOL_STATE:, OFF_XB:OFF_XB + POOL_WIDTH].astype(F32))
        v_s.append(zs3[:, :, OFF_V:OFF_V + SGU_WIDTH].astype(F32))

        mk = _norm_matmul(mem2, g_mem[l], w_mk[l].astype(BF16), None, ((0, 1, "identity"),), F32,
                          MEM_WIDTH, name="mem_k")
        mv = _norm_matmul(mem2, g_mem[l], w_mv[l].astype(BF16), None, ((0, 1, "identity"),), F32,
                          MEM_WIDTH, name="mem_v")
        mk_p.append(mk.reshape(batch, N_MEM, MEM_HEADS, MEM_HDIM))
        mv_p.append(mv.reshape(batch, N_MEM, MEM_HEADS, MEM_HDIM))

        zp = _in_proj(xp, g_mix[l], w_in_b, b_gate[l], False, "in_proj_prompt")
        tm_p = min(MIXER_ROWS, seq)
        mp = _mixer(zp, zp, True, mk.reshape(batch, N_MEM, MEM_WIDTH), mv.reshape(batch, N_MEM, MEM_WIDTH),
                    gv, ws, jnp.transpose(bs), pscale, *branch_weights,
                    n_seq=1, seq_rows=tm_p, chunk_len=SGU_CHUNK, pos0=0, tiles_per_seq=seq // tm_p,
                    name="mixer_prompt")
        x1p = _out_proj(xp, mp, wo_b, "out_proj_prompt")
        xp = _ffn(x1p, g_ffn[l], (wgu_b, wd_b), g_final, cast_w=False, name="ffn_prompt")
        xb_p = zp.reshape(batch, seq, -1)[:, seq - POOL_STATE:, OFF_XB:OFF_XB + POOL_WIDTH]
        pool_p.append(xb_p.astype(F32))

    y_prompt = xp.reshape(batch, seq, d)
    y_sample = xs.reshape(dec_batch, dec_seq, d)
    return (y_prompt, y_sample, jnp.stack(pool_p), jnp.stack(pool_s),
            jnp.stack(mk_p), jnp.stack(mv_p), jnp.stack(v_s))
```

```python
import functools

import jax
import jax.numpy as jnp
from jax import lax
from jax.experimental import pallas as pl
from jax.experimental.pallas import tpu as pltpu

F32 = jnp.float32
BF16 = jnp.bfloat16

EPS = 1e-6
CHUNK_SHIFT = 6
SGU_CHUNK = 128
SGU_WIDTH = 1024
SGU_GROUPS = 4
SGU_GDIM = SGU_WIDTH // SGU_GROUPS
POOL_WIDTH = 1024
POOL_WINDOWS = (2, 4, 8, 16)
POOL_GDIM = POOL_WIDTH // len(POOL_WINDOWS)
POOL_STATE = max(POOL_WINDOWS) - 1
N_MEM = 256
MEM_HEADS = 4
MEM_HDIM = 256
MEM_WIDTH = MEM_HEADS * MEM_HDIM
PAST_LEN = 1024
OFF_V = SGU_WIDTH
OFF_XB = 2 * SGU_WIDTH
OFF_Q = OFF_XB + POOL_WIDTH
OFF_GATES = OFF_Q + MEM_WIDTH

MIB = 1024 * 1024

IN_ROWS = 1024
IN_COLS = 2048
IN_COLS_CAST = 1024
MIXER_ROWS = 512
MIXER_SAMPLE_SEQS = 4
OUT_ROWS = 1024
OUT_MIN_STEPS = 4
FFN_ROWS = 1024
FFN_COLS = 512
FFN_COLS_CAST = 256
ROW_CHUNK = 512
VMEM_LIMIT = 57 * MIB


def _rms(x, gain):
    ms = jnp.mean(x * x, axis=-1, keepdims=True)
    return (x * lax.rsqrt(ms + EPS)) * gain


def _gelu_tanh(x):
    cdf = 0.5 * (1.0 + jnp.tanh(0.7978845608028654 * (x + 0.044715 * (x * x * x))))
    return x * cdf


def _sigmoid(x):
    return 0.5 + 0.5 * jnp.tanh(0.5 * x)


def _row_chunks(rows):
    step = min(ROW_CHUNK, rows)
    return [slice(r, r + step) for r in range(0, rows, step)]


def _norm_matmul_kernel(*refs, epilogues, bias_block0, cast_w):
    refs = list(refs)
    h_ref = refs.pop()
    wb_ref = refs.pop() if cast_w else None
    o_ref = refs.pop()
    b_ref = refs.pop() if bias_block0 is not None else None
    x_ref, g_ref, w_ref = refs
    n = pl.program_id(1)

    def block(kind, with_norm):
        if cast_w:
            wb_ref[0] = w_ref[...].astype(BF16)
        w_src = wb_ref if cast_w else w_ref
        for rs in _row_chunks(x_ref.shape[0]):
            if with_norm:
                h_ref[rs, :] = _rms(x_ref[rs, :], g_ref[...]).astype(BF16)
            z = jnp.dot(h_ref[rs, :], w_src[0], preferred_element_type=F32)
            if kind == "gelu":
                z = _gelu_tanh(z)
            elif kind == "sigmoid_bias":
                z = _sigmoid(z + b_ref[n - bias_block0])
            o_ref[0, rs, :] = z.astype(o_ref.dtype)

    for lo, hi, kind in epilogues:
        if lo == 0:
            pl.when(n == 0)(functools.partial(block, kind, True))
            lo = 1
        if hi > lo:
            pl.when((n >= lo) & (n < hi))(functools.partial(block, kind, False))


def _norm_matmul(x2, gain, w, bias, epilogues, out_dtype, slab, bias_block0=None, cast_w=False,
                 name="norm_matmul"):
    m, k = x2.shape
    bm = min(IN_ROWS, m)
    if cast_w:
        n_cols, bn = w.shape[1], IN_COLS_CAST
        w_spec = pl.BlockSpec((k, bn), lambda i, j: (0, j))
    else:
        n_cols, bn = w.shape[0] * slab, slab
        w_spec = pl.BlockSpec((1, k, bn), lambda i, j: (j, 0, 0))
    per_slab = slab // bn
    grid = (m // bm, n_cols // bn)
    assert not cast_w or grid[0] == 1, "the bf16 weight copy is written once per column block"
    in_specs = [
        pl.BlockSpec((bm, k), lambda i, j: (i, 0)),
        pl.BlockSpec((1, k), lambda i, j: (0, 0)),
        w_spec,
    ]
    args = [x2, gain.reshape(1, k), w]
    if bias is not None:
        bias3 = bias.reshape(-1, 1, bn)
        in_specs.append(pl.BlockSpec(bias3.shape, lambda i, j: (0, 0, 0)))
        args.append(bias3)
    out_specs = [pl.BlockSpec((1, bm, bn), lambda i, j: (j // per_slab, i, j % per_slab))]
    out_shape = [jax.ShapeDtypeStruct((n_cols // slab, m, slab), out_dtype)]
    if cast_w:
        out_specs.append(pl.BlockSpec((1, k, bn), lambda i, j: (j // per_slab, 0, j % per_slab)))
        out_shape.append(jax.ShapeDtypeStruct((n_cols // slab, k, slab), BF16))
    out = pl.pallas_call(
        functools.partial(_norm_matmul_kernel, epilogues=epilogues,
                          bias_block0=bias_block0 if bias is not None else None, cast_w=cast_w),
        grid=grid,
        in_specs=in_specs,
        out_specs=out_specs,
        out_shape=out_shape,
        scratch_shapes=[pltpu.VMEM((bm, k), BF16)],
        compiler_params=pltpu.CompilerParams(
            dimension_semantics=("arbitrary", "arbitrary"),
            vmem_limit_bytes=VMEM_LIMIT),
        name=name,
    )(*args)
    return tuple(out) if cast_w else out[0]


def _in_proj(x2, gain, w, bias, cast_w, name):
    bn = IN_COLS_CAST if cast_w else IN_COLS
    n_cols = OFF_GATES + bias.shape[0]
    epilogues = ((0, OFF_XB // bn, "gelu"), (OFF_XB // bn, OFF_GATES // bn, "identity"),
                 (OFF_GATES // bn, n_cols // bn, "sigmoid_bias"))
    return _norm_matmul(x2, gain, w, bias, epilogues, BF16, IN_COLS, bias_block0=OFF_GATES // bn, cast_w=cast_w,
                        name=name)


def _mixer_kernel(uv_ref, xq_ref, ga_ref, gb_ref, gc_ref, pre_ref,
                  mk_ref, mv_ref, gv_ref, ws_ref, bs_ref, ps_ref, wpool_ref,
                  wpa_ref, wpb_ref, wpc_ref,
                  o_ref, sgu_scr, pool_scr, att_scr,
                  *, n_seq, seq_rows, chunk_len, pool_chunk, pos0, tiles_per_seq, zero_first_prefix):
    i = pl.program_id(0)
    tile_in_seq = i % tiles_per_seq
    tile_rows = n_seq * seq_rows

    ri = lax.broadcasted_iota(jnp.int32, (chunk_len, chunk_len), 0)
    cj = lax.broadcasted_iota(jnp.int32, (chunk_len, chunk_len), 1)
    causal = jnp.right_shift(cj, CHUNK_SHIFT) <= jnp.right_shift(ri, CHUNK_SHIFT)
    ws = [jnp.where(causal, ws_ref[g], 0.0).astype(BF16) for g in range(SGU_GROUPS)]
    bs_full = [jnp.broadcast_to(bs_ref[:, g:g + 1], (chunk_len, SGU_GDIM)) for g in range(SGU_GROUPS)]

    bt = lax.broadcasted_iota(jnp.int32, (pool_chunk, 2 * pool_chunk), 0)
    bj = lax.broadcasted_iota(jnp.int32, (pool_chunk, 2 * pool_chunk), 1)
    back = bt + pool_chunk - bj
    bands = [jnp.where(back >= 0, jnp.where(back < w, 1.0, 0.0), 0.0).astype(BF16) for w in POOL_WINDOWS]
    trow = lax.broadcasted_iota(jnp.int32, (pool_chunk, 1), 0)

    keep_prefix = jnp.where(tile_in_seq == 0, 0.0, 1.0) if zero_first_prefix else None

    for s in range(n_seq):
        r0 = s * seq_rows

        v = uv_ref[0, r0:r0 + seq_rows, OFF_V:OFF_V + SGU_WIDTH].astype(F32)
        vn = _rms(v, gv_ref[...]).astype(BF16)
        for c in range(seq_rows // chunk_len):
            c0 = c * chunk_len
            for g in range(SGU_GROUPS):
                lo, hi = g * SGU_GDIM, (g + 1) * SGU_GDIM
                mixed = jnp.dot(ws[g], vn[c0:c0 + chunk_len, lo:hi], preferred_element_type=F32) + bs_full[g]
                u = uv_ref[0, r0 + c0:r0 + c0 + chunk_len, lo:hi].astype(F32)
                sgu_scr[r0 + c0:r0 + c0 + chunk_len, lo:hi] = (u * mixed).astype(BF16)

        for c in range(seq_rows // pool_chunk):
            c0 = r0 + c * pool_chunk
            pos = pos0 + tile_in_seq * tile_rows + c * pool_chunk + trow
            for g, w in enumerate(POOL_WINDOWS):
                lo, hi = g * POOL_GDIM, (g + 1) * POOL_GDIM
                cur = xq_ref[0, c0:c0 + pool_chunk, lo:hi]
                if c == 0:
                    prev = pre_ref[0, s * pool_chunk:(s + 1) * pool_chunk, lo:hi]
                    if zero_first_prefix:
                        prev = (prev.astype(F32) * keep_prefix).astype(BF16)
                else:
                    prev = xq_ref[0, c0 - pool_chunk:c0, lo:hi]
                window = jnp.concatenate([prev, cur], axis=0)
                wsum = jnp.dot(bands[g], window, preferred_element_type=F32)
                cnt = jnp.minimum(pos + 1, w).astype(F32)
                pooled = wsum / cnt - cur.astype(F32)
                pg = jnp.dot(pooled.astype(BF16), wpool_ref[g], preferred_element_type=F32)
                pool_scr[c0:c0 + pool_chunk, lo:hi] = (pg * ps_ref[:, lo:hi]).astype(BF16)

        kb = mk_ref[s].astype(BF16)
        vb = mv_ref[s].astype(BF16)
        for h in range(MEM_HEADS):
            lo, hi = h * MEM_HDIM, (h + 1) * MEM_HDIM
            qh = xq_ref[0, r0:r0 + seq_rows, POOL_WIDTH + lo:POOL_WIDTH + hi]
            sc = lax.dot_general(qh, kb[:, lo:hi], (((1,), (1,)), ((), ())),
                                 preferred_element_type=F32) * (MEM_HDIM ** -0.5)
            e = jnp.exp(sc - jnp.max(sc, axis=-1, keepdims=True))
            p = (e / jnp.sum(e, axis=-1, keepdims=True)).astype(BF16)
            att_scr[r0:r0 + seq_rows, lo:hi] = jnp.dot(
                p, vb[:, lo:hi], preferred_element_type=F32).astype(BF16)

    merged = ga_ref[0].astype(F32) * jnp.dot(sgu_scr[...], wpa_ref[...], preferred_element_type=F32)
    merged += gb_ref[0].astype(F32) * jnp.dot(pool_scr[...], wpb_ref[...], preferred_element_type=F32)
    merged += gc_ref[0].astype(F32) * jnp.dot(att_scr[...], wpc_ref[...], preferred_element_type=F32)
    o_ref[...] = merged.astype(BF16)


def _mixer(z, prefix, prefix_in_z, mem_k, mem_v, gv, ws, bs_t, pscale, wpool, wpa, wpb, wpc,
           *, n_seq, seq_rows, chunk_len, pos0, tiles_per_seq, name):
    m = z.shape[1]
    d = wpa.shape[1]
    assert d == IN_COLS and OFF_XB == IN_COLS and OFF_GATES == 2 * IN_COLS
    tm = n_seq * seq_rows
    pool_chunk = min(SGU_CHUNK, seq_rows)
    grid = (m // tm,)
    const2 = lambda i: (0, 0)
    const3 = lambda i: (0, 0, 0)
    single = pl.Buffered(1)
    if prefix_in_z:
        blocks_per_tile = tm // pool_chunk
        pre_spec = pl.BlockSpec((1, pool_chunk, POOL_WIDTH),
                                lambda i: (1, jnp.maximum(i * blocks_per_tile - 1, 0), 0))
        mem_idx = lambda i: (i // tiles_per_seq, 0, 0)
    else:
        pre_spec = pl.BlockSpec((1, n_seq * pool_chunk, POOL_WIDTH), lambda i: (0, i, 0))
        mem_idx = lambda i: (i, 0, 0)
    slab = lambda k: pl.BlockSpec((1, tm, d), lambda i: (k, i, 0))
    in_specs = [
        slab(0),
        slab(1),
        slab(2), slab(3), slab(4),
        pre_spec,
        pl.BlockSpec((n_seq, N_MEM, MEM_WIDTH), mem_idx),
        pl.BlockSpec((n_seq, N_MEM, MEM_WIDTH), mem_idx),
        pl.BlockSpec((1, SGU_WIDTH), const2),
        pl.BlockSpec((SGU_GROUPS, chunk_len, chunk_len), const3),
        pl.BlockSpec((chunk_len, SGU_GROUPS), const2),
        pl.BlockSpec((1, POOL_WIDTH), const2),
        pl.BlockSpec((len(POOL_WINDOWS), POOL_GDIM, POOL_GDIM), const3, pipeline_mode=single),
        pl.BlockSpec((SGU_WIDTH, d), const2, pipeline_mode=single),
        pl.BlockSpec((POOL_WIDTH, d), const2, pipeline_mode=single),
        pl.BlockSpec((MEM_WIDTH, d), const2, pipeline_mode=single),
    ]
    kern = functools.partial(
        _mixer_kernel, n_seq=n_seq, seq_rows=seq_rows, chunk_len=chunk_len, pool_chunk=pool_chunk,
        pos0=pos0, tiles_per_seq=tiles_per_seq, zero_first_prefix=prefix_in_z)
    return pl.pallas_call(
        kern,
        grid=grid,
        in_specs=in_specs,
        out_specs=pl.BlockSpec((tm, d), lambda i: (i, 0)),
        out_shape=jax.ShapeDtypeStruct((m, d), BF16),
        scratch_shapes=[pltpu.VMEM((tm, SGU_WIDTH), BF16),
                        pltpu.VMEM((tm, POOL_WIDTH), BF16),
                        pltpu.VMEM((tm, MEM_WIDTH), BF16)],
        compiler_params=pltpu.CompilerParams(
            dimension_semantics=("arbitrary",),
            vmem_limit_bytes=VMEM_LIMIT),
        name=name,
    )(z, z, z, z, z, prefix, mem_k, mem_v, gv.reshape(1, -1), ws, bs_t,
      pscale.reshape(1, -1), wpool, wpa, wpb, wpc)


def _out_proj_kernel(x_ref, m_ref, w_ref, o_ref):
    for rs in _row_chunks(x_ref.shape[0]):
        o_ref[rs, :] = x_ref[rs, :] + jnp.dot(m_ref[rs, :], w_ref[...], preferred_element_type=F32)


def _out_proj(x2, merged, wo, name):
    m, d = x2.shape
    tm = min(OUT_ROWS, m // OUT_MIN_STEPS)
    row = lambda i: (i, 0)
    return pl.pallas_call(
        _out_proj_kernel,
        grid=(m // tm,),
        in_specs=[pl.BlockSpec((tm, d), row), pl.BlockSpec((tm, d), row),
                  pl.BlockSpec((d, d), lambda i: (0, 0), pipeline_mode=pl.Buffered(1))],
        out_specs=pl.BlockSpec((tm, d), row),
        out_shape=jax.ShapeDtypeStruct((m, d), F32),
        compiler_params=pltpu.CompilerParams(
            dimension_semantics=("arbitrary",),
            vmem_limit_bytes=VMEM_LIMIT),
        name=name,
    )(x2, merged, wo)


def _ffn_kernel(*refs, n_chunks, cast_w):
    refs = list(refs)
    h_ref = refs.pop()
    if cast_w:
        wdb_ref, wgub_ref = refs.pop(), refs.pop()
        o_ref = refs.pop()
        x_ref, g_ref, wg_ref, wu_ref, wd_ref, gf_ref = refs
        wgub_ref[0, :, :FFN_COLS_CAST] = wg_ref[...].astype(BF16)
        wgub_ref[0, :, FFN_COLS_CAST:] = wu_ref[...].astype(BF16)
        wdb_ref[...] = wd_ref[...].astype(BF16)
        wgu_ref, wd_ref = wgub_ref, wdb_ref
    else:
        o_ref = refs.pop()
        x_ref, g_ref, wgu_ref, wd_ref, gf_ref = refs
    f = pl.program_id(1)
    groups = wd_ref.shape[0] // FFN_COLS_CAST

    def step(first, last):
        for rs in _row_chunks(x_ref.shape[0]):
            if first:
                h_ref[rs, :] = _rms(x_ref[rs, :], g_ref[...]).astype(BF16)
            gu = jnp.dot(h_ref[rs, :], wgu_ref[0], preferred_element_type=F32)
            acts = []
            for p in range(groups):
                gate = gu[:, (2 * p) * FFN_COLS_CAST:(2 * p + 1) * FFN_COLS_CAST]
                up = gu[:, (2 * p + 1) * FFN_COLS_CAST:(2 * p + 2) * FFN_COLS_CAST]
                acts.append(((gate * _sigmoid(gate)) * up).astype(BF16))
            act = acts[0] if groups == 1 else jnp.concatenate(acts, axis=1)
            acc = x_ref[rs, :] if first else o_ref[rs, :]
            y = acc + jnp.dot(act, wd_ref[...], preferred_element_type=F32)
            o_ref[rs, :] = _rms(y, gf_ref[...]) if last else y

    pl.when(f == 0)(functools.partial(step, True, False))
    pl.when((f > 0) & (f < n_chunks - 1))(functools.partial(step, False, False))
    pl.when(f == n_chunks - 1)(functools.partial(step, False, True))


def _ffn(x2, g_ffn, weights, g_final, *, cast_w, name):
    m, d = x2.shape
    d_ff = weights[-1].shape[0]
    tm = min(FFN_ROWS, m)
    assert not cast_w or m == tm, "the bf16 weight copies are written once per d_ff chunk"
    fc = FFN_COLS_CAST if cast_w else FFN_COLS
    n_chunks = d_ff // fc
    row = lambda i, f: (i, 0)
    vec = lambda i, f: (0, 0)
    cols = pl.BlockSpec((d, fc), lambda i, f: (0, f))
    rows = pl.BlockSpec((fc, d), lambda i, f: (f, 0))
    out_specs = [pl.BlockSpec((tm, d), row)]
    out_shape = [jax.ShapeDtypeStruct((m, d), F32)]
    if cast_w:
        per_slab = FFN_COLS // fc
        w_specs = [cols, cols, rows]
        out_specs += [pl.BlockSpec((1, d, 2 * fc), lambda i, f: (f // per_slab, 0, f % per_slab)), rows]
        out_shape += [jax.ShapeDtypeStruct((d_ff // FFN_COLS, d, 2 * FFN_COLS), BF16),
                      jax.ShapeDtypeStruct((d_ff, d), BF16)]
    else:
        w_specs = [pl.BlockSpec((1, d, 2 * fc), lambda i, f: (f, 0, 0)), rows]
    out = pl.pallas_call(
        functools.partial(_ffn_kernel, n_chunks=n_chunks, cast_w=cast_w),
        grid=(m // tm, n_chunks),
        in_specs=[pl.BlockSpec((tm, d), row), pl.BlockSpec((1, d), vec)] + w_specs + [pl.BlockSpec((1, d), vec)],
        out_specs=out_specs,
        out_shape=out_shape,
        scratch_shapes=[pltpu.VMEM((tm, d), BF16)],
        compiler_params=pltpu.CompilerParams(
            dimension_semantics=("arbitrary", "arbitrary"),
            vmem_limit_bytes=VMEM_LIMIT),
        name=name,
    )(x2, g_ffn.reshape(1, d), *weights, g_final.reshape(1, d))
    return tuple(out) if cast_w else out[0]


def kernel(x_prompt, x_sample, mem_prompt, state_pool, cache_mem_k, cache_mem_v, g_mix, w_in, b_gate, g_sgu_v, w_sgu, b_sgu, w_pool, pool_scale, g_mem, w_mk, w_mv, w_pa, w_pb, w_pc, w_o, g_ffn, w_ff_gate, w_ff_up, w_ff_down, g_final):
    depth = g_mix.shape[0]
    batch, seq, d = x_prompt.shape
    dec_batch, dec_seq, _ = x_sample.shape
    assert depth == 1, "final norm is fused into the last layer's FFN call"
    assert d == IN_COLS and dec_seq >= POOL_STATE

    xp = x_prompt.reshape(batch * seq, d)
    xs = x_sample.reshape(dec_batch * dec_seq, d)
    mem2 = mem_prompt.reshape(batch * N_MEM, d)

    pool_p, pool_s, mk_p, mv_p, v_s = [], [], [], [], []
    for l in range(depth):
        wpool_b = w_pool[l].astype(BF16)
        wpa_b, wpb_b, wpc_b, wo_b = (w[l].astype(BF16) for w in (w_pa, w_pb, w_pc, w_o))
        branch_weights = (wpool_b, wpa_b, wpb_b, wpc_b)
        gv, ws, bs, pscale = g_sgu_v[l], w_sgu[l], b_sgu[l], pool_scale[l]

        zs, w_in_b = _in_proj(xs, g_mix[l], w_in[l], b_gate[l], True, "in_proj_sample")
        prefix = jnp.pad(state_pool[l], ((0, 0), (dec_seq - POOL_STATE, 0), (0, 0)))
        prefix = prefix.reshape(1, dec_batch * dec_seq, POOL_WIDTH).astype(BF16)
        ms = _mixer(zs, prefix, False,
                    cache_mem_k[l].reshape(dec_batch, N_MEM, MEM_WIDTH),
                    cache_mem_v[l].reshape(dec_batch, N_MEM, MEM_WIDTH),
                    gv, ws[:, :dec_seq, :dec_seq], jnp.transpose(bs[:, :dec_seq]), pscale, *branch_weights,
                    n_seq=min(MIXER_SAMPLE_SEQS, dec_batch), seq_rows=dec_seq, chunk_len=dec_seq,
                    pos0=PAST_LEN, tiles_per_seq=1, name="mixer_sample")
        x1s = _out_proj(xs, ms, wo_b, "out_proj_sample")
        xs, wgu_b, wd_b = _ffn(x1s, g_ffn[l], (w_ff_gate[l], w_ff_up[l], w_ff_down[l]), g_final,
                               cast_w=True, name="ffn_sample")
        xb_s = zs[OFF_XB // IN_COLS].reshape(dec_batch, dec_seq, IN_COLS)[:, dec_seq - POOL_STATE:, :POOL_WIDTH]
        pool_s.append(xb_s.astype(F32))
        v_s.append(zs[0].reshape(dec_batch, dec_seq, IN_COLS)[:, :, OFF_V:OFF_V + SGU_WIDTH].astype(F32))

        mk = _norm_matmul(mem2, g_mem[l], w_mk[l].astype(BF16)[None], None, ((0, 1, "identity"),), F32,
                          MEM_WIDTH, name="mem_k")[0]
        mv = _norm_matmul(mem2, g_mem[l], w_mv[l].astype(BF16)[None], None, ((0, 1, "identity"),), F32,
                          MEM_WIDTH, name="mem_v")[0]
        mk_p.append(mk.reshape(batch, N_MEM, MEM_HEADS, MEM_HDIM))
        mv_p.append(mv.reshape(batch, N_MEM, MEM_HEADS, MEM_HDIM))

        zp = _in_proj(xp, g_mix[l], w_in_b, b_gate[l], False, "in_proj_prompt")
        tm_p = min(MIXER_ROWS, seq)
        mp = _mixer(zp, zp, True, mk.reshape(batch, N_MEM, MEM_WIDTH), mv.reshape(batch, N_MEM, MEM_WIDTH),
                    gv, ws, jnp.transpose(bs), pscale, *branch_weights,
                    n_seq=1, seq_rows=tm_p, chunk_len=SGU_CHUNK, pos0=0, tiles_per_seq=seq // tm_p,
                    name="mixer_prompt")
        x1p = _out_proj(xp, mp, wo_b, "out_proj_prompt")
        xp = _ffn(x1p, g_ffn[l], (wgu_b, wd_b), g_final, cast_w=False, name="ffn_prompt")
        xb_p = zp[OFF_XB // IN_COLS].reshape(batch, seq, IN_COLS)[:, seq - POOL_STATE:, :POOL_WIDTH]
        pool_p.append(xb_p.astype(F32))

    y_prompt = xp.reshape(batch, seq, d)
    y_sample = xs.reshape(dec_batch, dec_seq, d)
    return (y_prompt, y_sample, jnp.stack(pool_p), jnp.stack(pool_s),
            jnp.stack(mk_p), jnp.stack(mv_p), jnp.stack(v_s))
```

```python
import functools

import jax
import jax.numpy as jnp
from jax import lax
from jax.experimental import pallas as pl
from jax.experimental.pallas import tpu as pltpu

F32 = jnp.float32
BF16 = jnp.bfloat16

EPS = 1e-6
CHUNK_SHIFT = 6
SGU_CHUNK = 128
SGU_WIDTH = 1024
SGU_GROUPS = 4
SGU_GDIM = SGU_WIDTH // SGU_GROUPS
POOL_WIDTH = 1024
POOL_WINDOWS = (2, 4, 8, 16)
POOL_GDIM = POOL_WIDTH // len(POOL_WINDOWS)
POOL_STATE = max(POOL_WINDOWS) - 1
N_MEM = 256
MEM_HEADS = 4
MEM_HDIM = 256
MEM_WIDTH = MEM_HEADS * MEM_HDIM
PAST_LEN = 1024
OFF_XB = 2 * SGU_WIDTH
OFF_GATES = OFF_XB + POOL_WIDTH + MEM_WIDTH

MIB = 1024 * 1024

IN_ROWS = 1024
IN_COLS = 2048
IN_COLS_CAST = 1024
MIXER_ROWS = 512
MIXER_SAMPLE_SEQS = 4
OUT_ROWS = 1024
OUT_MIN_STEPS = 4
FFN_ROWS = 1024
FFN_COLS = 512
FFN_COLS_CAST = 256
ROW_CHUNK = 512
VMEM_LIMIT = 57 * MIB


def _rms(x, gain):
    ms = jnp.mean(x * x, axis=-1, keepdims=True)
    return (x * lax.rsqrt(ms + EPS)) * gain


def _gelu_tanh(x):
    cdf = 0.5 * (1.0 + jnp.tanh(0.7978845608028654 * (x + 0.044715 * (x * x * x))))
    return x * cdf


def _sigmoid(x):
    return 0.5 + 0.5 * jnp.tanh(0.5 * x)


def _row_chunks(rows):
    step = min(ROW_CHUNK, rows)
    return [slice(r, r + step) for r in range(0, rows, step)]


def _norm_matmul_kernel(*refs, epilogues, has_bias, cast_w):
    refs = list(refs)
    h_ref = refs.pop()
    wb_ref = refs.pop() if cast_w else None
    o_ref = refs.pop()
    b_ref = refs.pop() if has_bias else None
    x_ref, g_ref, w_ref = refs
    n = pl.program_id(1)

    def block(kind, with_norm):
        if cast_w:
            wb_ref[...] = w_ref[...].astype(BF16)
        w_src = wb_ref if cast_w else w_ref
        for rs in _row_chunks(x_ref.shape[0]):
            if with_norm:
                h_ref[rs, :] = _rms(x_ref[rs, :], g_ref[...]).astype(BF16)
            z = jnp.dot(h_ref[rs, :], w_src[...], preferred_element_type=F32)
            if kind == "gelu":
                z = _gelu_tanh(z)
            elif kind == "sigmoid_bias":
                z = _sigmoid(z + b_ref[...])
            o_ref[rs, :] = z.astype(o_ref.dtype)

    for lo, hi, kind in epilogues:
        if lo == 0:
            pl.when(n == 0)(functools.partial(block, kind, True))
            lo = 1
        if hi > lo:
            pl.when((n >= lo) & (n < hi))(functools.partial(block, kind, False))


def _norm_matmul(x2, gain, w, bias, epilogues, out_dtype, bn, bias_block0=0, cast_w=False, name="norm_matmul"):
    m, k = x2.shape
    n_cols = w.shape[1]
    bm = min(IN_ROWS, m)
    grid = (m // bm, n_cols // bn)
    assert not cast_w or grid[0] == 1, "the bf16 weight copy is written once per column block"
    in_specs = [
        pl.BlockSpec((bm, k), lambda i, j: (i, 0)),
        pl.BlockSpec((1, k), lambda i, j: (0, 0)),
        pl.BlockSpec((k, bn), lambda i, j: (0, j)),
    ]
    args = [x2, gain.reshape(1, k), w]
    if bias is not None:
        in_specs.append(pl.BlockSpec((1, bn), lambda i, j: (0, jnp.maximum(j - bias_block0, 0))))
        args.append(bias.reshape(1, -1))
    out_specs = [pl.BlockSpec((bm, bn), lambda i, j: (i, j))]
    out_shape = [jax.ShapeDtypeStruct((m, n_cols), out_dtype)]
    if cast_w:
        out_specs.append(pl.BlockSpec((k, bn), lambda i, j: (0, j)))
        out_shape.append(jax.ShapeDtypeStruct((k, n_cols), BF16))
    out = pl.pallas_call(
        functools.partial(_norm_matmul_kernel, epilogues=epilogues, has_bias=bias is not None, cast_w=cast_w),
        grid=grid,
        in_specs=in_specs,
        out_specs=out_specs,
        out_shape=out_shape,
        scratch_shapes=[pltpu.VMEM((bm, k), BF16)],
        compiler_params=pltpu.CompilerParams(
            dimension_semantics=("arbitrary", "arbitrary"),
            vmem_limit_bytes=VMEM_LIMIT),
        name=name,
    )(*args)
    return tuple(out) if cast_w else out[0]


def _in_proj(x2, gain, w, bias, cast_w, name):
    bn = IN_COLS_CAST if cast_w else IN_COLS
    epilogues = ((0, OFF_XB // bn, "gelu"), (OFF_XB // bn, OFF_GATES // bn, "identity"),
                 (OFF_GATES // bn, w.shape[1] // bn, "sigmoid_bias"))
    return _norm_matmul(x2, gain, w, bias, epilogues, BF16, bn, bias_block0=OFF_GATES // bn, cast_w=cast_w,
                        name=name)


def _mixer_kernel(u_ref, v_ref, xb_ref, q_ref, ga_ref, gb_ref, gc_ref, pre_ref,
                  mk_ref, mv_ref, gv_ref, ws_ref, bs_ref, ps_ref, wpool_ref,
                  wpa_ref, wpb_ref, wpc_ref,
                  o_ref, sgu_scr, pool_scr, att_scr,
                  *, n_seq, seq_rows, chunk_len, pool_chunk, pos0, tiles_per_seq, zero_first_prefix):
    i = pl.program_id(0)
    tile_in_seq = i % tiles_per_seq
    tile_rows = n_seq * seq_rows

    ri = lax.broadcasted_iota(jnp.int32, (chunk_len, chunk_len), 0)
    cj = lax.broadcasted_iota(jnp.int32, (chunk_len, chunk_len), 1)
    causal = jnp.right_shift(cj, CHUNK_SHIFT) <= jnp.right_shift(ri, CHUNK_SHIFT)
    ws = [jnp.where(causal, ws_ref[g], 0.0).astype(BF16) for g in range(SGU_GROUPS)]
    bs_full = [jnp.broadcast_to(bs_ref[:, g:g + 1], (chunk_len, SGU_GDIM)) for g in range(SGU_GROUPS)]

    bt = lax.broadcasted_iota(jnp.int32, (pool_chunk, 2 * pool_chunk), 0)
    bj = lax.broadcasted_iota(jnp.int32, (pool_chunk, 2 * pool_chunk), 1)
    back = bt + pool_chunk - bj
    bands = [jnp.where(back >= 0, jnp.where(back < w, 1.0, 0.0), 0.0).astype(BF16) for w in POOL_WINDOWS]
    trow = lax.broadcasted_iota(jnp.int32, (pool_chunk, 1), 0)

    keep_prefix = jnp.where(tile_in_seq == 0, 0.0, 1.0) if zero_first_prefix else None

    for s in range(n_seq):
        r0 = s * seq_rows

        v = v_ref[r0:r0 + seq_rows, :].astype(F32)
        vn = _rms(v, gv_ref[...]).astype(BF16)
        for c in range(seq_rows // chunk_len):
            c0 = c * chunk_len
            for g in range(SGU_GROUPS):
                lo, hi = g * SGU_GDIM, (g + 1) * SGU_GDIM
                mixed = jnp.dot(ws[g], vn[c0:c0 + chunk_len, lo:hi], preferred_element_type=F32) + bs_full[g]
                u = u_ref[r0 + c0:r0 + c0 + chunk_len, lo:hi].astype(F32)
                sgu_scr[r0 + c0:r0 + c0 + chunk_len, lo:hi] = (u * mixed).astype(BF16)

        for c in range(seq_rows // pool_chunk):
            c0 = r0 + c * pool_chunk
            pos = pos0 + tile_in_seq * tile_rows + c * pool_chunk + trow
            for g, w in enumerate(POOL_WINDOWS):
                lo, hi = g * POOL_GDIM, (g + 1) * POOL_GDIM
                cur = xb_ref[c0:c0 + pool_chunk, lo:hi]
                if c == 0:
                    prev = pre_ref[s * pool_chunk:(s + 1) * pool_chunk, lo:hi]
                    if zero_first_prefix:
                        prev = (prev.astype(F32) * keep_prefix).astype(BF16)
                else:
                    prev = xb_ref[c0 - pool_chunk:c0, lo:hi]
                window = jnp.concatenate([prev, cur], axis=0)
                wsum = jnp.dot(bands[g], window, preferred_element_type=F32)
                cnt = jnp.minimum(pos + 1, w).astype(F32)
                pooled = wsum / cnt - cur.astype(F32)
                pg = jnp.dot(pooled.astype(BF16), wpool_ref[g], preferred_element_type=F32)
                pool_scr[c0:c0 + pool_chunk, lo:hi] = (pg * ps_ref[:, lo:hi]).astype(BF16)

        kb = mk_ref[s].astype(BF16)
        vb = mv_ref[s].astype(BF16)
        for h in range(MEM_HEADS):
            lo, hi = h * MEM_HDIM, (h + 1) * MEM_HDIM
            qh = q_ref[r0:r0 + seq_rows, lo:hi]
            sc = lax.dot_general(qh, kb[:, lo:hi], (((1,), (1,)), ((), ())),
                                 preferred_element_type=F32) * (MEM_HDIM ** -0.5)
            e = jnp.exp(sc - jnp.max(sc, axis=-1, keepdims=True))
            p = (e / jnp.sum(e, axis=-1, keepdims=True)).astype(BF16)
            att_scr[r0:r0 + seq_rows, lo:hi] = jnp.dot(
                p, vb[:, lo:hi], preferred_element_type=F32).astype(BF16)

    merged = ga_ref[...].astype(F32) * jnp.dot(sgu_scr[...], wpa_ref[...], preferred_element_type=F32)
    merged += gb_ref[...].astype(F32) * jnp.dot(pool_scr[...], wpb_ref[...], preferred_element_type=F32)
    merged += gc_ref[...].astype(F32) * jnp.dot(att_scr[...], wpc_ref[...], preferred_element_type=F32)
    o_ref[...] = merged.astype(BF16)


def _mixer(z, prefix, prefix_in_z, mem_k, mem_v, gv, ws, bs_t, pscale, wpool, wpa, wpb, wpc,
           *, n_seq, seq_rows, chunk_len, pos0, tiles_per_seq, name):
    m = z.shape[0]
    d = wpa.shape[1]
    tm = n_seq * seq_rows
    pool_chunk = min(SGU_CHUNK, seq_rows)
    grid = (m // tm,)
    const2 = lambda i: (0, 0)
    const3 = lambda i: (0, 0, 0)
    single = pl.Buffered(1)
    gate_block0 = OFF_GATES // d
    if prefix_in_z:
        blocks_per_tile = tm // pool_chunk
        pre_spec = pl.BlockSpec((pool_chunk, POOL_WIDTH),
                                lambda i: (jnp.maximum(i * blocks_per_tile - 1, 0), OFF_XB // POOL_WIDTH))
        mem_idx = lambda i: (i // tiles_per_seq, 0, 0)
    else:
        pre_spec = pl.BlockSpec((n_seq * pool_chunk, POOL_WIDTH), lambda i: (i, 0))
        mem_idx = lambda i: (i, 0, 0)
    in_specs = [
        pl.BlockSpec((tm, SGU_WIDTH), lambda i: (i, 0)),
        pl.BlockSpec((tm, SGU_WIDTH), lambda i: (i, 1)),
        pl.BlockSpec((tm, POOL_WIDTH), lambda i: (i, 2)),
        pl.BlockSpec((tm, MEM_WIDTH), lambda i: (i, 3)),
        pl.BlockSpec((tm, d), lambda i: (i, gate_block0)),
        pl.BlockSpec((tm, d), lambda i: (i, gate_block0 + 1)),
        pl.BlockSpec((tm, d), lambda i: (i, gate_block0 + 2)),
        pre_spec,
        pl.BlockSpec((n_seq, N_MEM, MEM_WIDTH), mem_idx),
        pl.BlockSpec((n_seq, N_MEM, MEM_WIDTH), mem_idx),
        pl.BlockSpec((1, SGU_WIDTH), const2),
        pl.BlockSpec((SGU_GROUPS, chunk_len, chunk_len), const3),
        pl.BlockSpec((chunk_len, SGU_GROUPS), const2),
        pl.BlockSpec((1, POOL_WIDTH), const2),
        pl.BlockSpec((len(POOL_WINDOWS), POOL_GDIM, POOL_GDIM), const3, pipeline_mode=single),
        pl.BlockSpec((SGU_WIDTH, d), const2, pipeline_mode=single),
        pl.BlockSpec((POOL_WIDTH, d), const2, pipeline_mode=single),
        pl.BlockSpec((MEM_WIDTH, d), const2, pipeline_mode=single),
    ]
    kern = functools.partial(
        _mixer_kernel, n_seq=n_seq, seq_rows=seq_rows, chunk_len=chunk_len, pool_chunk=pool_chunk,
        pos0=pos0, tiles_per_seq=tiles_per_seq, zero_first_prefix=prefix_in_z)
    return pl.pallas_call(
        kern,
        grid=grid,
        in_specs=in_specs,
        out_specs=pl.BlockSpec((tm, d), lambda i: (i, 0)),
        out_shape=jax.ShapeDtypeStruct((m, d), BF16),
        scratch_shapes=[pltpu.VMEM((tm, SGU_WIDTH), BF16),
                        pltpu.VMEM((tm, POOL_WIDTH), BF16),
                        pltpu.VMEM((tm, MEM_WIDTH), BF16)],
        compiler_params=pltpu.CompilerParams(
            dimension_semantics=("arbitrary",),
            vmem_limit_bytes=VMEM_LIMIT),
        name=name,
    )(z, z, z, z, z, z, z, prefix, mem_k, mem_v, gv.reshape(1, -1), ws, bs_t,
      pscale.reshape(1, -1), wpool, wpa, wpb, wpc)


def _out_proj_kernel(x_ref, m_ref, w_ref, o_ref):
    for rs in _row_chunks(x_ref.shape[0]):
        o_ref[rs, :] = x_ref[rs, :] + jnp.dot(m_ref[rs, :], w_ref[...], preferred_element_type=F32)


def _out_proj(x2, merged, wo, name):
    m, d = x2.shape
    tm = min(OUT_ROWS, m // OUT_MIN_STEPS)
    row = lambda i: (i, 0)
    return pl.pallas_call(
        _out_proj_kernel,
        grid=(m // tm,),
        in_specs=[pl.BlockSpec((tm, d), row), pl.BlockSpec((tm, d), row),
                  pl.BlockSpec((d, d), lambda i: (0, 0), pipeline_mode=pl.Buffered(1))],
        out_specs=pl.BlockSpec((tm, d), row),
        out_shape=jax.ShapeDtypeStruct((m, d), F32),
        compiler_params=pltpu.CompilerParams(
            dimension_semantics=("arbitrary",),
            vmem_limit_bytes=VMEM_LIMIT),
        name=name,
    )(x2, merged, wo)


def _ffn_kernel(*refs, n_chunks, cast_w):
    refs = list(refs)
    h_ref = refs.pop()
    if cast_w:
        wdb_ref, wub_ref, wgb_ref = refs.pop(), refs.pop(), refs.pop()
    o_ref = refs.pop()
    x_ref, g_ref, wg_ref, wu_ref, wd_ref, gf_ref = refs
    f = pl.program_id(1)

    if cast_w:
        wgb_ref[...] = wg_ref[...].astype(BF16)
        wub_ref[...] = wu_ref[...].astype(BF16)
        wdb_ref[...] = wd_ref[...].astype(BF16)
        wg_ref, wu_ref, wd_ref = wgb_ref, wub_ref, wdb_ref

    def step(first, last):
        for rs in _row_chunks(x_ref.shape[0]):
            if first:
                h_ref[rs, :] = _rms(x_ref[rs, :], g_ref[...]).astype(BF16)
            h = h_ref[rs, :]
            gate = jnp.dot(h, wg_ref[...], preferred_element_type=F32)
            up = jnp.dot(h, wu_ref[...], preferred_element_type=F32)
            act = ((gate * _sigmoid(gate)) * up).astype(BF16)
            acc = x_ref[rs, :] if first else o_ref[rs, :]
            y = acc + jnp.dot(act, wd_ref[...], preferred_element_type=F32)
            o_ref[rs, :] = _rms(y, gf_ref[...]) if last else y

    pl.when(f == 0)(functools.partial(step, True, False))
    pl.when((f > 0) & (f < n_chunks - 1))(functools.partial(step, False, False))
    pl.when(f == n_chunks - 1)(functools.partial(step, False, True))


def _ffn(x2, g_ffn, wg, wu, wd, g_final, *, cast_w, name):
    m, d = x2.shape
    d_ff = wg.shape[1]
    tm = min(FFN_ROWS, m)
    assert not cast_w or m == tm, "the bf16 weight copies are written once per d_ff chunk"
    fc = FFN_COLS_CAST if cast_w else FFN_COLS
    n_chunks = d_ff // fc
    row = lambda i, f: (i, 0)
    vec = lambda i, f: (0, 0)
    cols = pl.BlockSpec((d, fc), lambda i, f: (0, f))
    rows = pl.BlockSpec((fc, d), lambda i, f: (f, 0))
    out_specs = [pl.BlockSpec((tm, d), row)]
    out_shape = [jax.ShapeDtypeStruct((m, d), F32)]
    if cast_w:
        out_specs += [cols, cols, rows]
        out_shape += [jax.ShapeDtypeStruct(w.shape, BF16) for w in (wg, wu, wd)]
    out = pl.pallas_call(
        functools.partial(_ffn_kernel, n_chunks=n_chunks, cast_w=cast_w),
        grid=(m // tm, n_chunks),
        in_specs=[pl.BlockSpec((tm, d), row), pl.BlockSpec((1, d), vec), cols, cols, rows,
                  pl.BlockSpec((1, d), vec)],
        out_specs=out_specs,
        out_shape=out_shape,
        scratch_shapes=[pltpu.VMEM((tm, d), BF16)],
        compiler_params=pltpu.CompilerParams(
            dimension_semantics=("arbitrary", "arbitrary"),
            vmem_limit_bytes=VMEM_LIMIT),
        name=name,
    )(x2, g_ffn.reshape(1, d), wg, wu, wd, g_final.reshape(1, d))
    return tuple(out) if cast_w else out[0]


def kernel(x_prompt, x_sample, mem_prompt, state_pool, cache_mem_k, cache_mem_v, g_mix, w_in, b_gate, g_sgu_v, w_sgu, b_sgu, w_pool, pool_scale, g_mem, w_mk, w_mv, w_pa, w_pb, w_pc, w_o, g_ffn, w_ff_gate, w_ff_up, w_ff_down, g_final):
    depth = g_mix.shape[0]
    batch, seq, d = x_prompt.shape
    dec_batch, dec_seq, _ = x_sample.shape
    assert depth == 1, "final norm is fused into the last layer's FFN call"
    assert d == IN_COLS and dec_seq >= POOL_STATE

    xp = x_prompt.reshape(batch * seq, d)
    xs = x_sample.reshape(dec_batch * dec_seq, d)
    mem2 = mem_prompt.reshape(batch * N_MEM, d)

    pool_p, pool_s, mk_p, mv_p, v_s = [], [], [], [], []
    for l in range(depth):
        wpool_b = w_pool[l].astype(BF16)
        wpa_b, wpb_b, wpc_b, wo_b = (w[l].astype(BF16) for w in (w_pa, w_pb, w_pc, w_o))
        branch_weights = (wpool_b, wpa_b, wpb_b, wpc_b)
        gv, ws, bs, pscale = g_sgu_v[l], w_sgu[l], b_sgu[l], pool_scale[l]

        zs, w_in_b = _in_proj(xs, g_mix[l], w_in[l], b_gate[l], True, "in_proj_sample")
        prefix = jnp.pad(state_pool[l], ((0, 0), (dec_seq - POOL_STATE, 0), (0, 0)))
        prefix = prefix.reshape(dec_batch * dec_seq, POOL_WIDTH).astype(BF16)
        ms = _mixer(zs, prefix, False,
                    cache_mem_k[l].reshape(dec_batch, N_MEM, MEM_WIDTH),
                    cache_mem_v[l].reshape(dec_batch, N_MEM, MEM_WIDTH),
                    gv, ws[:, :dec_seq, :dec_seq], jnp.transpose(bs[:, :dec_seq]), pscale, *branch_weights,
                    n_seq=min(MIXER_SAMPLE_SEQS, dec_batch), seq_rows=dec_seq, chunk_len=dec_seq,
                    pos0=PAST_LEN, tiles_per_seq=1, name="mixer_sample")
        x1s = _out_proj(xs, ms, wo_b, "out_proj_sample")
        xs, wg_b, wu_b, wd_b = _ffn(x1s, g_ffn[l], w_ff_gate[l], w_ff_up[l], w_ff_down[l], g_final,
                                    cast_w=True, name="ffn_sample")
        zs3 = zs.reshape(dec_batch, dec_seq, -1)
        pool_s.append(zs3[:, dec_seq - POOL_STATE:, OFF_XB:OFF_XB + POOL_WIDTH].astype(F32))
        v_s.append(zs3[:, :, SGU_WIDTH:2 * SGU_WIDTH].astype(F32))

        mk = _norm_matmul(mem2, g_mem[l], w_mk[l].astype(BF16), None, ((0, 1, "identity"),), F32,
                          MEM_WIDTH, name="mem_k")
        mv = _norm_matmul(mem2, g_mem[l], w_mv[l].astype(BF16), None, ((0, 1, "identity"),), F32,
                          MEM_WIDTH, name="mem_v")
        mk_p.append(mk.reshape(batch, N_MEM, MEM_HEADS, MEM_HDIM))
        mv_p.append(mv.reshape(batch, N_MEM, MEM_HEADS, MEM_HDIM))

        zp = _in_proj(xp, g_mix[l], w_in_b, b_gate[l], False, "in_proj_prompt")
        tm_p = min(MIXER_ROWS, seq)
        mp = _mixer(zp, zp, True, mk.reshape(batch, N_MEM, MEM_WIDTH), mv.reshape(batch, N_MEM, MEM_WIDTH),
                    gv, ws, jnp.transpose(bs), pscale, *branch_weights,
                    n_seq=1, seq_rows=tm_p, chunk_len=SGU_CHUNK, pos0=0, tiles_per_seq=seq // tm_p,
                    name="mixer_prompt")
        x1p = _out_proj(xp, mp, wo_b, "out_proj_prompt")
        xp = _ffn(x1p, g_ffn[l], wg_b, wu_b, wd_b, g_final, cast_w=False, name="ffn_prompt")
        xb_p = zp.reshape(batch, seq, -1)[:, seq - POOL_STATE:, OFF_XB:OFF_XB + POOL_WIDTH]
        pool_p.append(xb_p.astype(F32))

    y_prompt = xp.reshape(batch, seq, d)
    y_sample = xs.reshape(dec_batch, dec_seq, d)
    return (y_prompt, y_sample, jnp.stack(pool_p), jnp.stack(pool_s),
            jnp.stack(mk_p), jnp.stack(mv_p), jnp.stack(v_s))
```

```python
import functools

import jax
import jax.numpy as jnp
from jax import lax
from jax.experimental import pallas as pl
from jax.experimental.pallas import tpu as pltpu

F32 = jnp.float32
BF16 = jnp.bfloat16

EPS = 1e-6
CHUNK_SHIFT = 6
SGU_CHUNK = 128
SGU_WIDTH = 1024
SGU_GROUPS = 4
SGU_GDIM = SGU_WIDTH // SGU_GROUPS
POOL_WIDTH = 1024
POOL_WINDOWS = (2, 4, 8, 16)
POOL_GDIM = POOL_WIDTH // len(POOL_WINDOWS)
POOL_STATE = max(POOL_WINDOWS) - 1
N_MEM = 256
MEM_HEADS = 4
MEM_HDIM = 256
MEM_WIDTH = MEM_HEADS * MEM_HDIM
PAST_LEN = 1024
OFF_XB = 2 * SGU_WIDTH
OFF_GATES = OFF_XB + POOL_WIDTH + MEM_WIDTH

MIB = 1024 * 1024

IN_ROWS = 1024
IN_COLS = 2048
IN_COLS_CAST = 1024
MIXER_ROWS = 512
MIXER_SAMPLE_SEQS = 4
OUT_ROWS = 1024
OUT_MIN_STEPS = 4
FFN_ROWS = 1024
FFN_COLS = 512
FFN_COLS_CAST = 256
ROW_CHUNK = 512
VMEM_LIMIT = 57 * MIB


def _rms(x, gain):
    ms = jnp.mean(x * x, axis=-1, keepdims=True)
    return (x * lax.rsqrt(ms + EPS)) * gain


def _gelu_tanh(x):
    cdf = 0.5 * (1.0 + jnp.tanh(0.7978845608028654 * (x + 0.044715 * (x * x * x))))
    return x * cdf


def _sigmoid(x):
    return 0.5 + 0.5 * jnp.tanh(0.5 * x)


def _row_chunks(rows):
    step = min(ROW_CHUNK, rows)
    return [slice(r, r + step) for r in range(0, rows, step)]


def _norm_matmul_kernel(*refs, epilogues, has_bias, cast_w):
    refs = list(refs)
    h_ref = refs.pop()
    wb_ref = refs.pop() if cast_w else None
    o_ref = refs.pop()
    b_ref = refs.pop() if has_bias else None
    x_ref, g_ref, w_ref = refs
    n = pl.program_id(1)

    def block(kind, with_norm):
        if cast_w:
            wb_ref[...] = w_ref[...].astype(BF16)
        w_src = wb_ref if cast_w else w_ref
        for rs in _row_chunks(x_ref.shape[0]):
            if with_norm:
                h_ref[rs, :] = _rms(x_ref[rs, :], g_ref[...]).astype(BF16)
            z = jnp.dot(h_ref[rs, :], w_src[...], preferred_element_type=F32)
            if kind == "gelu":
                z = _gelu_tanh(z)
            elif kind == "sigmoid_bias":
                z = _sigmoid(z + b_ref[...])
            o_ref[rs, :] = z.astype(o_ref.dtype)

    for lo, hi, kind in epilogues:
        if lo == 0:
            pl.when(n == 0)(functools.partial(block, kind, True))
            lo = 1
        if hi > lo:
            pl.when((n >= lo) & (n < hi))(functools.partial(block, kind, False))


def _norm_matmul(x2, gain, w, bias, epilogues, out_dtype, bn, bias_block0=0, cast_w=False, name="norm_matmul"):
    m, k = x2.shape
    n_cols = w.shape[1]
    bm = min(IN_ROWS, m)
    grid = (m // bm, n_cols // bn)
    assert not cast_w or grid[0] == 1, "the bf16 weight copy is written once per column block"
    in_specs = [
        pl.BlockSpec((bm, k), lambda i, j: (i, 0)),
        pl.BlockSpec((1, k), lambda i, j: (0, 0)),
        pl.BlockSpec((k, bn), lambda i, j: (0, j)),
    ]
    args = [x2, gain.reshape(1, k), w]
    if bias is not None:
        in_specs.append(pl.BlockSpec((1, bn), lambda i, j: (0, jnp.maximum(j - bias_block0, 0))))
        args.append(bias.reshape(1, -1))
    out_specs = [pl.BlockSpec((bm, bn), lambda i, j: (i, j))]
    out_shape = [jax.ShapeDtypeStruct((m, n_cols), out_dtype)]
    if cast_w:
        out_specs.append(pl.BlockSpec((k, bn), lambda i, j: (0, j)))
        out_shape.append(jax.ShapeDtypeStruct((k, n_cols), BF16))
    out = pl.pallas_call(
        functools.partial(_norm_matmul_kernel, epilogues=epilogues, has_bias=bias is not None, cast_w=cast_w),
        grid=grid,
        in_specs=in_specs,
        out_specs=out_specs,
        out_shape=out_shape,
        scratch_shapes=[pltpu.VMEM((bm, k), BF16)],
        compiler_params=pltpu.CompilerParams(
            dimension_semantics=("arbitrary", "arbitrary"),
            vmem_limit_bytes=VMEM_LIMIT),
        name=name,
    )(*args)
    return tuple(out) if cast_w else out[0]


def _in_proj(x2, gain, w, bias, cast_w, name):
    bn = IN_COLS_CAST if cast_w else IN_COLS
    epilogues = ((0, OFF_XB // bn, "gelu"), (OFF_XB // bn, OFF_GATES // bn, "identity"),
                 (OFF_GATES // bn, w.shape[1] // bn, "sigmoid_bias"))
    return _norm_matmul(x2, gain, w, bias, epilogues, BF16, bn, bias_block0=OFF_GATES // bn, cast_w=cast_w,
                        name=name)


def _mixer_kernel(u_ref, v_ref, xb_ref, q_ref, ga_ref, gb_ref, gc_ref, pre_ref,
                  mk_ref, mv_ref, gv_ref, ws_ref, bs_ref, ps_ref, wpool_ref,
                  wpa_ref, wpb_ref, wpc_ref,
                  o_ref, sgu_scr, pool_scr, att_scr,
                  *, n_seq, seq_rows, chunk_len, pool_chunk, pos0, tiles_per_seq, zero_first_prefix):
    i = pl.program_id(0)
    tile_in_seq = i % tiles_per_seq
    tile_rows = n_seq * seq_rows

    ri = lax.broadcasted_iota(jnp.int32, (chunk_len, chunk_len), 0)
    cj = lax.broadcasted_iota(jnp.int32, (chunk_len, chunk_len), 1)
    causal = jnp.right_shift(cj, CHUNK_SHIFT) <= jnp.right_shift(ri, CHUNK_SHIFT)
    ws = [jnp.where(causal, ws_ref[g], 0.0).astype(BF16) for g in range(SGU_GROUPS)]
    bs_full = [jnp.broadcast_to(bs_ref[:, g:g + 1], (chunk_len, SGU_GDIM)) for g in range(SGU_GROUPS)]

    bt = lax.broadcasted_iota(jnp.int32, (pool_chunk, 2 * pool_chunk), 0)
    bj = lax.broadcasted_iota(jnp.int32, (pool_chunk, 2 * pool_chunk), 1)
    back = bt + pool_chunk - bj
    bands = [jnp.where(back >= 0, jnp.where(back < w, 1.0, 0.0), 0.0).astype(BF16) for w in POOL_WINDOWS]
    trow = lax.broadcasted_iota(jnp.int32, (pool_chunk, 1), 0)

    keep_prefix = jnp.where(tile_in_seq == 0, 0.0, 1.0) if zero_first_prefix else None

    assert chunk_len == pool_chunk
    chunk_rows = [(s, c, s * seq_rows + c * chunk_len) for s in range(n_seq) for c in range(seq_rows // chunk_len)]

    vn = [_rms(v_ref[s * seq_rows:(s + 1) * seq_rows, :].astype(F32), gv_ref[...]).astype(BF16)
          for s in range(n_seq)]
    for g in range(SGU_GROUPS):
        lo, hi = g * SGU_GDIM, (g + 1) * SGU_GDIM
        v_side = jnp.concatenate([vn[s][c * chunk_len:(c + 1) * chunk_len, lo:hi] for s, c, _ in chunk_rows], axis=1)
        mixed = jnp.dot(ws[g], v_side, preferred_element_type=F32)
        for k, (_, _, r) in enumerate(chunk_rows):
            u = u_ref[r:r + chunk_len, lo:hi].astype(F32)
            mixed_k = mixed[:, k * SGU_GDIM:(k + 1) * SGU_GDIM] + bs_full[g]
            sgu_scr[r:r + chunk_len, lo:hi] = (u * mixed_k).astype(BF16)

    for g, w in enumerate(POOL_WINDOWS):
        lo, hi = g * POOL_GDIM, (g + 1) * POOL_GDIM
        windows = []
        for s, c, r in chunk_rows:
            if c == 0:
                prev = pre_ref[s * pool_chunk:(s + 1) * pool_chunk, lo:hi]
                if zero_first_prefix:
                    prev = (prev.astype(F32) * keep_prefix).astype(BF16)
            else:
                prev = xb_ref[r - pool_chunk:r, lo:hi]
            windows.append(jnp.concatenate([prev, xb_ref[r:r + pool_chunk, lo:hi]], axis=0))
        wsum = jnp.dot(bands[g], jnp.concatenate(windows, axis=1), preferred_element_type=F32)
        pooled = []
        for k, (_, c, r) in enumerate(chunk_rows):
            pos = pos0 + tile_in_seq * tile_rows + c * pool_chunk + trow
            cnt = jnp.minimum(pos + 1, w).astype(F32)
            cur = xb_ref[r:r + pool_chunk, lo:hi].astype(F32)
            pooled.append((wsum[:, k * POOL_GDIM:(k + 1) * POOL_GDIM] / cnt - cur).astype(BF16))
        pg = jnp.dot(jnp.concatenate(pooled, axis=0), wpool_ref[g], preferred_element_type=F32)
        pool_scr[:, lo:hi] = (pg * ps_ref[:, lo:hi]).astype(BF16)

    for s in range(n_seq):
        r0 = s * seq_rows

        kb = mk_ref[s].astype(BF16)
        vb = mv_ref[s].astype(BF16)
        for h in range(MEM_HEADS):
            lo, hi = h * MEM_HDIM, (h + 1) * MEM_HDIM
            qh = q_ref[r0:r0 + seq_rows, lo:hi]
            sc = lax.dot_general(qh, kb[:, lo:hi], (((1,), (1,)), ((), ())),
                                 preferred_element_type=F32) * (MEM_HDIM ** -0.5)
            e = jnp.exp(sc - jnp.max(sc, axis=-1, keepdims=True))
            p = (e / jnp.sum(e, axis=-1, keepdims=True)).astype(BF16)
            att_scr[r0:r0 + seq_rows, lo:hi] = jnp.dot(
                p, vb[:, lo:hi], preferred_element_type=F32).astype(BF16)

    merged = ga_ref[...].astype(F32) * jnp.dot(sgu_scr[...], wpa_ref[...], preferred_element_type=F32)
    merged += gb_ref[...].astype(F32) * jnp.dot(pool_scr[...], wpb_ref[...], preferred_element_type=F32)
    merged += gc_ref[...].astype(F32) * jnp.dot(att_scr[...], wpc_ref[...], preferred_element_type=F32)
    o_ref[...] = merged.astype(BF16)


def _mixer(z, prefix, prefix_in_z, mem_k, mem_v, gv, ws, bs_t, pscale, wpool, wpa, wpb, wpc,
           *, n_seq, seq_rows, chunk_len, pos0, tiles_per_seq, name):
    m = z.shape[0]
    d = wpa.shape[1]
    tm = n_seq * seq_rows
    pool_chunk = min(SGU_CHUNK, seq_rows)
    grid = (m // tm,)
    const2 = lambda i: (0, 0)
    const3 = lambda i: (0, 0, 0)
    single = pl.Buffered(1)
    gate_block0 = OFF_GATES // d
    if prefix_in_z:
        blocks_per_tile = tm // pool_chunk
        pre_spec = pl.BlockSpec((pool_chunk, POOL_WIDTH),
                                lambda i: (jnp.maximum(i * blocks_per_tile - 1, 0), OFF_XB // POOL_WIDTH))
        mem_idx = lambda i: (i // tiles_per_seq, 0, 0)
    else:
        pre_spec = pl.BlockSpec((n_seq * pool_chunk, POOL_WIDTH), lambda i: (i, 0))
        mem_idx = lambda i: (i, 0, 0)
    in_specs = [
        pl.BlockSpec((tm, SGU_WIDTH), lambda i: (i, 0)),
        pl.BlockSpec((tm, SGU_WIDTH), lambda i: (i, 1)),
        pl.BlockSpec((tm, POOL_WIDTH), lambda i: (i, 2)),
        pl.BlockSpec((tm, MEM_WIDTH), lambda i: (i, 3)),
        pl.BlockSpec((tm, d), lambda i: (i, gate_block0)),
        pl.BlockSpec((tm, d), lambda i: (i, gate_block0 + 1)),
        pl.BlockSpec((tm, d), lambda i: (i, gate_block0 + 2)),
        pre_spec,
        pl.BlockSpec((n_seq, N_MEM, MEM_WIDTH), mem_idx),
        pl.BlockSpec((n_seq, N_MEM, MEM_WIDTH), mem_idx),
        pl.BlockSpec((1, SGU_WIDTH), const2),
        pl.BlockSpec((SGU_GROUPS, chunk_len, chunk_len), const3),
        pl.BlockSpec((chunk_len, SGU_GROUPS), const2),
        pl.BlockSpec((1, POOL_WIDTH), const2),
        pl.BlockSpec((len(POOL_WINDOWS), POOL_GDIM, POOL_GDIM), const3, pipeline_mode=single),
        pl.BlockSpec((SGU_WIDTH, d), const2, pipeline_mode=single),
        pl.BlockSpec((POOL_WIDTH, d), const2, pipeline_mode=single),
        pl.BlockSpec((MEM_WIDTH, d), const2, pipeline_mode=single),
    ]
    kern = functools.partial(
        _mixer_kernel, n_seq=n_seq, seq_rows=seq_rows, chunk_len=chunk_len, pool_chunk=pool_chunk,
        pos0=pos0, tiles_per_seq=tiles_per_seq, zero_first_prefix=prefix_in_z)
    return pl.pallas_call(
        kern,
        grid=grid,
        in_specs=in_specs,
        out_specs=pl.BlockSpec((tm, d), lambda i: (i, 0)),
        out_shape=jax.ShapeDtypeStruct((m, d), BF16),
        scratch_shapes=[pltpu.VMEM((tm, SGU_WIDTH), BF16),
                        pltpu.VMEM((tm, POOL_WIDTH), BF16),
                        pltpu.VMEM((tm, MEM_WIDTH), BF16)],
        compiler_params=pltpu.CompilerParams(
            dimension_semantics=("arbitrary",),
            vmem_limit_bytes=VMEM_LIMIT),
        name=name,
    )(z, z, z, z, z, z, z, prefix, mem_k, mem_v, gv.reshape(1, -1), ws, bs_t,
      pscale.reshape(1, -1), wpool, wpa, wpb, wpc)


def _out_proj_kernel(x_ref, m_ref, w_ref, o_ref):
    for rs in _row_chunks(x_ref.shape[0]):
        o_ref[rs, :] = x_ref[rs, :] + jnp.dot(m_ref[rs, :], w_ref[...], preferred_element_type=F32)


def _out_proj(x2, merged, wo, name):
    m, d = x2.shape
    tm = min(OUT_ROWS, m // OUT_MIN_STEPS)
    row = lambda i: (i, 0)
    return pl.pallas_call(
        _out_proj_kernel,
        grid=(m // tm,),
        in_specs=[pl.BlockSpec((tm, d), row), pl.BlockSpec((tm, d), row),
                  pl.BlockSpec((d, d), lambda i: (0, 0), pipeline_mode=pl.Buffered(1))],
        out_specs=pl.BlockSpec((tm, d), row),
        out_shape=jax.ShapeDtypeStruct((m, d), F32),
        compiler_params=pltpu.CompilerParams(
            dimension_semantics=("arbitrary",),
            vmem_limit_bytes=VMEM_LIMIT),
        name=name,
    )(x2, merged, wo)


def _ffn_kernel(*refs, n_chunks, cast_w):
    refs = list(refs)
    h_ref = refs.pop()
    if cast_w:
        wdb_ref, wub_ref, wgb_ref = refs.pop(), refs.pop(), refs.pop()
    o_ref = refs.pop()
    x_ref, g_ref, wg_ref, wu_ref, wd_ref, gf_ref = refs
    f = pl.program_id(1)

    if cast_w:
        wgb_ref[...] = wg_ref[...].astype(BF16)
        wub_ref[...] = wu_ref[...].astype(BF16)
        wdb_ref[...] = wd_ref[...].astype(BF16)
        wg_ref, wu_ref, wd_ref = wgb_ref, wub_ref, wdb_ref

    def step(first, last):
        for rs in _row_chunks(x_ref.shape[0]):
            if first:
                h_ref[rs, :] = _rms(x_ref[rs, :], g_ref[...]).astype(BF16)
            h = h_ref[rs, :]
            gate = jnp.dot(h, wg_ref[...], preferred_element_type=F32)
            up = jnp.dot(h, wu_ref[...], preferred_element_type=F32)
            act = ((gate * _sigmoid(gate)) * up).astype(BF16)
            acc = x_ref[rs, :] if first else o_ref[rs, :]
            y = acc + jnp.dot(act, wd_ref[...], preferred_element_type=F32)
            o_ref[rs, :] = _rms(y, gf_ref[...]) if last else y

    pl.when(f == 0)(functools.partial(step, True, False))
    pl.when((f > 0) & (f < n_chunks - 1))(functools.partial(step, False, False))
    pl.when(f == n_chunks - 1)(functools.partial(step, False, True))


def _ffn(x2, g_ffn, wg, wu, wd, g_final, *, cast_w, name):
    m, d = x2.shape
    d_ff = wg.shape[1]
    tm = min(FFN_ROWS, m)
    assert not cast_w or m == tm, "the bf16 weight copies are written once per d_ff chunk"
    fc = FFN_COLS_CAST if cast_w else FFN_COLS
    n_chunks = d_ff // fc
    row = lambda i, f: (i, 0)
    vec = lambda i, f: (0, 0)
    cols = pl.BlockSpec((d, fc), lambda i, f: (0, f))
    rows = pl.BlockSpec((fc, d), lambda i, f: (f, 0))
    out_specs = [pl.BlockSpec((tm, d), row)]
    out_shape = [jax.ShapeDtypeStruct((m, d), F32)]
    if cast_w:
        out_specs += [cols, cols, rows]
        out_shape += [jax.ShapeDtypeStruct(w.shape, BF16) for w in (wg, wu, wd)]
    out = pl.pallas_call(
        functools.partial(_ffn_kernel, n_chunks=n_chunks, cast_w=cast_w),
        grid=(m // tm, n_chunks),
        in_specs=[pl.BlockSpec((tm, d), row), pl.BlockSpec((1, d), vec), cols, cols, rows,
                  pl.BlockSpec((1, d), vec)],
        out_specs=out_specs,
        out_shape=out_shape,
        scratch_shapes=[pltpu.VMEM((tm, d), BF16)],
        compiler_params=pltpu.CompilerParams(
            dimension_semantics=("arbitrary", "arbitrary"),
            vmem_limit_bytes=VMEM_LIMIT),
        name=name,
    )(x2, g_ffn.reshape(1, d), wg, wu, wd, g_final.reshape(1, d))
    return tuple(out) if cast_w else out[0]


def kernel(x_prompt, x_sample, mem_prompt, state_pool, cache_mem_k, cache_mem_v, g_mix, w_in, b_gate, g_sgu_v, w_sgu, b_sgu, w_pool, pool_scale, g_mem, w_mk, w_mv, w_pa, w_pb, w_pc, w_o, g_ffn, w_ff_gate, w_ff_up, w_ff_down, g_final):
    depth = g_mix.shape[0]
    batch, seq, d = x_prompt.shape
    dec_batch, dec_seq, _ = x_sample.shape
    assert depth == 1, "final norm is fused into the last layer's FFN call"
    assert d == IN_COLS and dec_seq >= POOL_STATE

    xp = x_prompt.reshape(batch * seq, d)
    xs = x_sample.reshape(dec_batch * dec_seq, d)
    mem2 = mem_prompt.reshape(batch * N_MEM, d)

    pool_p, pool_s, mk_p, mv_p, v_s = [], [], [], [], []
    for l in range(depth):
        wpool_b = w_pool[l].astype(BF16)
        wpa_b, wpb_b, wpc_b, wo_b = (w[l].astype(BF16) for w in (w_pa, w_pb, w_pc, w_o))
        branch_weights = (wpool_b, wpa_b, wpb_b, wpc_b)
        gv, ws, bs, pscale = g_sgu_v[l], w_sgu[l], b_sgu[l], pool_scale[l]

        zs, w_in_b = _in_proj(xs, g_mix[l], w_in[l], b_gate[l], True, "in_proj_sample")
        prefix = jnp.pad(state_pool[l], ((0, 0), (dec_seq - POOL_STATE, 0), (0, 0)))
        prefix = prefix.reshape(dec_batch * dec_seq, POOL_WIDTH).astype(BF16)
        ms = _mixer(zs, prefix, False,
                    cache_mem_k[l].reshape(dec_batch, N_MEM, MEM_WIDTH),
                    cache_mem_v[l].reshape(dec_batch, N_MEM, MEM_WIDTH),
                    gv, ws[:, :dec_seq, :dec_seq], jnp.transpose(bs[:, :dec_seq]), pscale, *branch_weights,
                    n_seq=min(MIXER_SAMPLE_SEQS, dec_batch), seq_rows=dec_seq, chunk_len=dec_seq,
                    pos0=PAST_LEN, tiles_per_seq=1, name="mixer_sample")
        x1s = _out_proj(xs, ms, wo_b, "out_proj_sample")
        xs, wg_b, wu_b, wd_b = _ffn(x1s, g_ffn[l], w_ff_gate[l], w_ff_up[l], w_ff_down[l], g_final,
                                    cast_w=True, name="ffn_sample")
        zs3 = zs.reshape(dec_batch, dec_seq, -1)
        pool_s.append(zs3[:, dec_seq - POOL_STATE:, OFF_XB:OFF_XB + POOL_WIDTH].astype(F32))
        v_s.append(zs3[:, :, SGU_WIDTH:2 * SGU_WIDTH].astype(F32))

        mk = _norm_matmul(mem2, g_mem[l], w_mk[l].astype(BF16), None, ((0, 1, "identity"),), F32,
                          MEM_WIDTH, name="mem_k")
        mv = _norm_matmul(mem2, g_mem[l], w_mv[l].astype(BF16), None, ((0, 1, "identity"),), F32,
                          MEM_WIDTH, name="mem_v")
        mk_p.append(mk.reshape(batch, N_MEM, MEM_HEADS, MEM_HDIM))
        mv_p.append(mv.reshape(batch, N_MEM, MEM_HEADS, MEM_HDIM))

        zp = _in_proj(xp, g_mix[l], w_in_b, b_gate[l], False, "in_proj_prompt")
        tm_p = min(MIXER_ROWS, seq)
        mp = _mixer(zp, zp, True, mk.reshape(batch, N_MEM, MEM_WIDTH), mv.reshape(batch, N_MEM, MEM_WIDTH),
                    gv, ws, jnp.transpose(bs), pscale, *branch_weights,
                    n_seq=1, seq_rows=tm_p, chunk_len=SGU_CHUNK, pos0=0, tiles_per_seq=seq // tm_p,
                    name="mixer_prompt")
        x1p = _out_proj(xp, mp, wo_b, "out_proj_prompt")
        xp = _ffn(x1p, g_ffn[l], wg_b, wu_b, wd_b, g_final, cast_w=False, name="ffn_prompt")
        xb_p = zp.reshape(batch, seq, -1)[:, seq - POOL_STATE:, OFF_XB:OFF_XB + POOL_WIDTH]
        pool_p.append(xb_p.astype(F32))

    y_prompt = xp.reshape(batch, seq, d)
    y_sample = xs.reshape(dec_batch, dec_seq, d)
    return (y_prompt, y_sample, jnp.stack(pool_p), jnp.stack(pool_s),
            jnp.stack(mk_p), jnp.stack(mv_p), jnp.stack(v_s))
```

```python
import functools

import jax
import jax.numpy as jnp
from jax import lax
from jax.experimental import pallas as pl
from jax.experimental.pallas import tpu as pltpu

F32 = jnp.float32
BF16 = jnp.bfloat16

EPS = 1e-6
CHUNK_SHIFT = 6
SGU_CHUNK = 128
SGU_WIDTH = 1024
SGU_GROUPS = 4
SGU_GDIM = SGU_WIDTH // SGU_GROUPS
POOL_WIDTH = 1024
POOL_WINDOWS = (2, 4, 8, 16)
POOL_GDIM = POOL_WIDTH // len(POOL_WINDOWS)
POOL_STATE = max(POOL_WINDOWS) - 1
N_MEM = 256
MEM_HEADS = 4
MEM_HDIM = 256
MEM_WIDTH = MEM_HEADS * MEM_HDIM
PAST_LEN = 1024
OFF_XB = 2 * SGU_WIDTH
OFF_GATES = OFF_XB + POOL_WIDTH + MEM_WIDTH

MIB = 1024 * 1024

IN_ROWS = 1024
IN_COLS = 2048
IN_COLS_CAST = 1024
MEM_KV_BATCHES = 2
MIXER_ROWS = 512
MIXER_SAMPLE_SEQS = 4
OUT_ROWS = 1024
OUT_MIN_STEPS = 4
FFN_ROWS = 1024
FFN_COLS = 512
FFN_COLS_CAST = 256
ROW_CHUNK = 512
VMEM_LIMIT = 57 * MIB


def _rms(x, gain):
    ms = jnp.mean(x * x, axis=-1, keepdims=True)
    return (x * lax.rsqrt(ms + EPS)) * gain


def _gelu_tanh(x):
    cdf = 0.5 * (1.0 + jnp.tanh(0.7978845608028654 * (x + 0.044715 * (x * x * x))))
    return x * cdf


def _sigmoid(x):
    return 0.5 + 0.5 * jnp.tanh(0.5 * x)


def _row_chunks(rows):
    step = min(ROW_CHUNK, rows)
    return [slice(r, r + step) for r in range(0, rows, step)]


def _norm_matmul_kernel(*refs, epilogues, has_bias, cast_w):
    refs = list(refs)
    h_ref = refs.pop()
    wb_ref = refs.pop() if cast_w else None
    o_ref = refs.pop()
    b_ref = refs.pop() if has_bias else None
    x_ref, g_ref, w_ref = refs
    n = pl.program_id(1)

    def block(kind, with_norm):
        if cast_w:
            wb_ref[...] = w_ref[...].astype(BF16)
        w_src = wb_ref if cast_w else w_ref
        for rs in _row_chunks(x_ref.shape[0]):
            if with_norm:
                h_ref[rs, :] = _rms(x_ref[rs, :], g_ref[...]).astype(BF16)
            z = jnp.dot(h_ref[rs, :], w_src[...], preferred_element_type=F32)
            if kind == "gelu":
                z = _gelu_tanh(z)
            elif kind == "sigmoid_bias":
                z = _sigmoid(z + b_ref[...])
            o_ref[rs, :] = z.astype(o_ref.dtype)

    for lo, hi, kind in epilogues:
        if lo == 0:
            pl.when(n == 0)(functools.partial(block, kind, True))
            lo = 1
        if hi > lo:
            pl.when((n >= lo) & (n < hi))(functools.partial(block, kind, False))


def _norm_matmul(x2, gain, w, bias, epilogues, out_dtype, bn, bias_block0=0, cast_w=False, name="norm_matmul"):
    m, k = x2.shape
    n_cols = w.shape[1]
    bm = min(IN_ROWS, m)
    grid = (m // bm, n_cols // bn)
    assert not cast_w or grid[0] == 1, "the bf16 weight copy is written once per column block"
    in_specs = [
        pl.BlockSpec((bm, k), lambda i, j: (i, 0)),
        pl.BlockSpec((1, k), lambda i, j: (0, 0)),
        pl.BlockSpec((k, bn), lambda i, j: (0, j)),
    ]
    args = [x2, gain.reshape(1, k), w]
    if bias is not None:
        in_specs.append(pl.BlockSpec((1, bn), lambda i, j: (0, jnp.maximum(j - bias_block0, 0))))
        args.append(bias.reshape(1, -1))
    out_specs = [pl.BlockSpec((bm, bn), lambda i, j: (i, j))]
    out_shape = [jax.ShapeDtypeStruct((m, n_cols), out_dtype)]
    if cast_w:
        out_specs.append(pl.BlockSpec((k, bn), lambda i, j: (0, j)))
        out_shape.append(jax.ShapeDtypeStruct((k, n_cols), BF16))
    out = pl.pallas_call(
        functools.partial(_norm_matmul_kernel, epilogues=epilogues, has_bias=bias is not None, cast_w=cast_w),
        grid=grid,
        in_specs=in_specs,
        out_specs=out_specs,
        out_shape=out_shape,
        scratch_shapes=[pltpu.VMEM((bm, k), BF16)],
        compiler_params=pltpu.CompilerParams(
            dimension_semantics=("arbitrary", "arbitrary"),
            vmem_limit_bytes=VMEM_LIMIT),
        name=name,
    )(*args)
    return tuple(out) if cast_w else out[0]


def _in_proj(x2, gain, w, bias, cast_w, name):
    bn = IN_COLS_CAST if cast_w else IN_COLS
    epilogues = ((0, OFF_XB // bn, "gelu"), (OFF_XB // bn, OFF_GATES // bn, "identity"),
                 (OFF_GATES // bn, w.shape[1] // bn, "sigmoid_bias"))
    return _norm_matmul(x2, gain, w, bias, epilogues, BF16, bn, bias_block0=OFF_GATES // bn, cast_w=cast_w,
                        name=name)


def _mem_kv_kernel(x_ref, g_ref, wk_ref, wv_ref, k_ref, v_ref, kt_ref):
    h = _rms(x_ref[...], g_ref[...]).astype(BF16)
    k = jnp.dot(h, wk_ref[...], preferred_element_type=F32)
    k_ref[...] = k
    v_ref[...] = jnp.dot(h, wv_ref[...], preferred_element_type=F32)
    for b in range(kt_ref.shape[0]):
        for hd in range(MEM_HEADS):
            lo, hi = hd * MEM_HDIM, (hd + 1) * MEM_HDIM
            kt_ref[b, lo:hi, :] = k[b * N_MEM:(b + 1) * N_MEM, lo:hi].T.astype(BF16)


def _mem_kv(mem2, gain, wk, wv, name):
    m, d = mem2.shape
    bm = MEM_KV_BATCHES * N_MEM
    row = lambda i: (i, 0)
    const = lambda i: (0, 0)
    return pl.pallas_call(
        _mem_kv_kernel,
        grid=(m // bm,),
        in_specs=[pl.BlockSpec((bm, d), row), pl.BlockSpec((1, d), const),
                  pl.BlockSpec((d, MEM_WIDTH), const), pl.BlockSpec((d, MEM_WIDTH), const)],
        out_specs=[pl.BlockSpec((bm, MEM_WIDTH), row), pl.BlockSpec((bm, MEM_WIDTH), row),
                   pl.BlockSpec((MEM_KV_BATCHES, MEM_WIDTH, N_MEM), lambda i: (i, 0, 0))],
        out_shape=[jax.ShapeDtypeStruct((m, MEM_WIDTH), F32), jax.ShapeDtypeStruct((m, MEM_WIDTH), F32),
                   jax.ShapeDtypeStruct((m // N_MEM, MEM_WIDTH, N_MEM), BF16)],
        compiler_params=pltpu.CompilerParams(
            dimension_semantics=("arbitrary",),
            vmem_limit_bytes=VMEM_LIMIT),
        name=name,
    )(mem2, gain.reshape(1, d), wk, wv)


def _mixer_kernel(u_ref, v_ref, xb_ref, q_ref, ga_ref, gb_ref, gc_ref, pre_ref,
                  mk_ref, mv_ref, gv_ref, ws_ref, bs_ref, ps_ref, wpool_ref,
                  wpa_ref, wpb_ref, wpc_ref,
                  o_ref, sgu_scr, pool_scr, att_scr,
                  *, n_seq, seq_rows, chunk_len, pool_chunk, pos0, tiles_per_seq, zero_first_prefix):
    i = pl.program_id(0)
    tile_in_seq = i % tiles_per_seq
    tile_rows = n_seq * seq_rows

    ri = lax.broadcasted_iota(jnp.int32, (chunk_len, chunk_len), 0)
    cj = lax.broadcasted_iota(jnp.int32, (chunk_len, chunk_len), 1)
    causal = jnp.right_shift(cj, CHUNK_SHIFT) <= jnp.right_shift(ri, CHUNK_SHIFT)
    ws = [jnp.where(causal, ws_ref[g], 0.0).astype(BF16) for g in range(SGU_GROUPS)]
    bs_full = [jnp.broadcast_to(bs_ref[:, g:g + 1], (chunk_len, SGU_GDIM)) for g in range(SGU_GROUPS)]

    bt = lax.broadcasted_iota(jnp.int32, (pool_chunk, 2 * pool_chunk), 0)
    bj = lax.broadcasted_iota(jnp.int32, (pool_chunk, 2 * pool_chunk), 1)
    back = bt + pool_chunk - bj
    bands = [jnp.where(back >= 0, jnp.where(back < w, 1.0, 0.0), 0.0).astype(BF16) for w in POOL_WINDOWS]
    trow = lax.broadcasted_iota(jnp.int32, (pool_chunk, 1), 0)

    keep_prefix = jnp.where(tile_in_seq == 0, 0.0, 1.0) if zero_first_prefix else None

    for s in range(n_seq):
        r0 = s * seq_rows

        kb = mk_ref[s].astype(BF16)
        vb = mv_ref[s].astype(BF16)
        for h in range(MEM_HEADS):
            lo, hi = h * MEM_HDIM, (h + 1) * MEM_HDIM
            qh = q_ref[r0:r0 + seq_rows, lo:hi]
            sc = jnp.dot(qh, kb[lo:hi, :], preferred_element_type=F32) * (MEM_HDIM ** -0.5)
            e = jnp.exp(sc - jnp.max(sc, axis=-1, keepdims=True))
            p = (e / jnp.sum(e, axis=-1, keepdims=True)).astype(BF16)
            att_scr[r0:r0 + seq_rows, lo:hi] = jnp.dot(
                p, vb[:, lo:hi], preferred_element_type=F32).astype(BF16)

    assert chunk_len == pool_chunk
    chunk_rows = [(s, c, s * seq_rows + c * chunk_len) for s in range(n_seq) for c in range(seq_rows // chunk_len)]

    vn = [_rms(v_ref[s * seq_rows:(s + 1) * seq_rows, :].astype(F32), gv_ref[...]).astype(BF16)
          for s in range(n_seq)]
    for g in range(SGU_GROUPS):
        lo, hi = g * SGU_GDIM, (g + 1) * SGU_GDIM
        v_side = jnp.concatenate([vn[s][c * chunk_len:(c + 1) * chunk_len, lo:hi] for s, c, _ in chunk_rows], axis=1)
        mixed = jnp.dot(ws[g], v_side, preferred_element_type=F32)
        for k, (_, _, r) in enumerate(chunk_rows):
            u = u_ref[r:r + chunk_len, lo:hi].astype(F32)
            mixed_k = mixed[:, k * SGU_GDIM:(k + 1) * SGU_GDIM] + bs_full[g]
            sgu_scr[r:r + chunk_len, lo:hi] = (u * mixed_k).astype(BF16)

    for g, w in enumerate(POOL_WINDOWS):
        lo, hi = g * POOL_GDIM, (g + 1) * POOL_GDIM
        windows = []
        for s, c, r in chunk_rows:
            if c == 0:
                prev = pre_ref[s * pool_chunk:(s + 1) * pool_chunk, lo:hi]
                if zero_first_prefix:
                    prev = (prev.astype(F32) * keep_prefix).astype(BF16)
            else:
                prev = xb_ref[r - pool_chunk:r, lo:hi]
            windows.append(jnp.concatenate([prev, xb_ref[r:r + pool_chunk, lo:hi]], axis=0))
        wsum = jnp.dot(bands[g], jnp.concatenate(windows, axis=1), preferred_element_type=F32)
        pooled = []
        for k, (_, c, r) in enumerate(chunk_rows):
            pos = pos0 + tile_in_seq * tile_rows + c * pool_chunk + trow
            cnt = jnp.minimum(pos + 1, w).astype(F32)
            cur = xb_ref[r:r + pool_chunk, lo:hi].astype(F32)
            pooled.append((wsum[:, k * POOL_GDIM:(k + 1) * POOL_GDIM] / cnt - cur).astype(BF16))
        pg = jnp.dot(jnp.concatenate(pooled, axis=0), wpool_ref[g], preferred_element_type=F32)
        pool_scr[:, lo:hi] = (pg * ps_ref[:, lo:hi]).astype(BF16)

    merged = ga_ref[...].astype(F32) * jnp.dot(sgu_scr[...], wpa_ref[...], preferred_element_type=F32)
    merged += gb_ref[...].astype(F32) * jnp.dot(pool_scr[...], wpb_ref[...], preferred_element_type=F32)
    merged += gc_ref[...].astype(F32) * jnp.dot(att_scr[...], wpc_ref[...], preferred_element_type=F32)
    o_ref[...] = merged.astype(BF16)


def _mixer(z, prefix, prefix_in_z, mem_k, mem_v, gv, ws, bs_t, pscale, wpool, wpa, wpb, wpc,
           *, n_seq, seq_rows, chunk_len, pos0, tiles_per_seq, name):
    m = z.shape[0]
    d = wpa.shape[1]
    tm = n_seq * seq_rows
    pool_chunk = min(SGU_CHUNK, seq_rows)
    grid = (m // tm,)
    const2 = lambda i: (0, 0)
    const3 = lambda i: (0, 0, 0)
    single = pl.Buffered(1)
    gate_block0 = OFF_GATES // d
    if prefix_in_z:
        blocks_per_tile = tm // pool_chunk
        pre_spec = pl.BlockSpec((pool_chunk, POOL_WIDTH),
                                lambda i: (jnp.maximum(i * blocks_per_tile - 1, 0), OFF_XB // POOL_WIDTH))
        mem_idx = lambda i: (i // tiles_per_seq, 0, 0)
    else:
        pre_spec = pl.BlockSpec((n_seq * pool_chunk, POOL_WIDTH), lambda i: (i, 0))
        mem_idx = lambda i: (i, 0, 0)
    in_specs = [
        pl.BlockSpec((tm, SGU_WIDTH), lambda i: (i, 0)),
        pl.BlockSpec((tm, SGU_WIDTH), lambda i: (i, 1)),
        pl.BlockSpec((tm, POOL_WIDTH), lambda i: (i, 2)),
        pl.BlockSpec((tm, MEM_WIDTH), lambda i: (i, 3)),
        pl.BlockSpec((tm, d), lambda i: (i, gate_block0)),
        pl.BlockSpec((tm, d), lambda i: (i, gate_block0 + 1)),
        pl.BlockSpec((tm, d), lambda i: (i, gate_block0 + 2)),
        pre_spec,
        pl.BlockSpec((n_seq, MEM_WIDTH, N_MEM), mem_idx),
        pl.BlockSpec((n_seq, N_MEM, MEM_WIDTH), mem_idx),
        pl.BlockSpec((1, SGU_WIDTH), const2),
        pl.BlockSpec((SGU_GROUPS, chunk_len, chunk_len), const3),
        pl.BlockSpec((chunk_len, SGU_GROUPS), const2),
        pl.BlockSpec((1, POOL_WIDTH), const2),
        pl.BlockSpec((len(POOL_WINDOWS), POOL_GDIM, POOL_GDIM), const3, pipeline_mode=single),
        pl.BlockSpec((SGU_WIDTH, d), const2, pipeline_mode=single),
        pl.BlockSpec((POOL_WIDTH, d), const2, pipeline_mode=single),
        pl.BlockSpec((MEM_WIDTH, d), const2, pipeline_mode=single),
    ]
    kern = functools.partial(
        _mixer_kernel, n_seq=n_seq, seq_rows=seq_rows, chunk_len=chunk_len, pool_chunk=pool_chunk,
        pos0=pos0, tiles_per_seq=tiles_per_seq, zero_first_prefix=prefix_in_z)
    return pl.pallas_call(
        kern,
        grid=grid,
        in_specs=in_specs,
        out_specs=pl.BlockSpec((tm, d), lambda i: (i, 0)),
        out_shape=jax.ShapeDtypeStruct((m, d), BF16),
        scratch_shapes=[pltpu.VMEM((tm, SGU_WIDTH), BF16),
                        pltpu.VMEM((tm, POOL_WIDTH), BF16),
                        pltpu.VMEM((tm, MEM_WIDTH), BF16)],
        compiler_params=pltpu.CompilerParams(
            dimension_semantics=("arbitrary",),
            vmem_limit_bytes=VMEM_LIMIT),
        name=name,
    )(z, z, z, z, z, z, z, prefix, mem_k, mem_v, gv.reshape(1, -1), ws, bs_t,
      pscale.reshape(1, -1), wpool, wpa, wpb, wpc)


def _out_proj_kernel(x_ref, m_ref, w_ref, o_ref):
    for rs in _row_chunks(x_ref.shape[0]):
        o_ref[rs, :] = x_ref[rs, :] + jnp.dot(m_ref[rs, :], w_ref[...], preferred_element_type=F32)


def _out_proj(x2, merged, wo, name):
    m, d = x2.shape
    tm = min(OUT_ROWS, m // OUT_MIN_STEPS)
    row = lambda i: (i, 0)
    return pl.pallas_call(
        _out_proj_kernel,
        grid=(m // tm,),
        in_specs=[pl.BlockSpec((tm, d), row), pl.BlockSpec((tm, d), row),
                  pl.BlockSpec((d, d), lambda i: (0, 0), pipeline_mode=pl.Buffered(1))],
        out_specs=pl.BlockSpec((tm, d), row),
        out_shape=jax.ShapeDtypeStruct((m, d), F32),
        compiler_params=pltpu.CompilerParams(
            dimension_semantics=("arbitrary",),
            vmem_limit_bytes=VMEM_LIMIT),
        name=name,
    )(x2, merged, wo)


def _ffn_kernel(*refs, n_chunks, cast_w):
    refs = list(refs)
    h_ref = refs.pop()
    if cast_w:
        wdb_ref, wub_ref, wgb_ref = refs.pop(), refs.pop(), refs.pop()
    o_ref = refs.pop()
    x_ref, g_ref, wg_ref, wu_ref, wd_ref, gf_ref = refs
    f = pl.program_id(1)

    if cast_w:
        wgb_ref[...] = wg_ref[...].astype(BF16)
        wub_ref[...] = wu_ref[...].astype(BF16)
        wdb_ref[...] = wd_ref[...].astype(BF16)
        wg_ref, wu_ref, wd_ref = wgb_ref, wub_ref, wdb_ref

    def step(first, last):
        for rs in _row_chunks(x_ref.shape[0]):
            if first:
                h_ref[rs, :] = _rms(x_ref[rs, :], g_ref[...]).astype(BF16)
            h = h_ref[rs, :]
            gate = jnp.dot(h, wg_ref[...], preferred_element_type=F32)
            up = jnp.dot(h, wu_ref[...], preferred_element_type=F32)
            act = ((gate * _sigmoid(gate)) * up).astype(BF16)
            acc = x_ref[rs, :] if first else o_ref[rs, :]
            y = acc + jnp.dot(act, wd_ref[...], preferred_element_type=F32)
            o_ref[rs, :] = _rms(y, gf_ref[...]) if last else y

    pl.when(f == 0)(functools.partial(step, True, False))
    pl.when((f > 0) & (f < n_chunks - 1))(functools.partial(step, False, False))
    pl.when(f == n_chunks - 1)(functools.partial(step, False, True))


def _ffn(x2, g_ffn, wg, wu, wd, g_final, *, cast_w, name):
    m, d = x2.shape
    d_ff = wg.shape[1]
    tm = min(FFN_ROWS, m)
    assert not cast_w or m == tm, "the bf16 weight copies are written once per d_ff chunk"
    fc = FFN_COLS_CAST if cast_w else FFN_COLS
    n_chunks = d_ff // fc
    row = lambda i, f: (i, 0)
    vec = lambda i, f: (0, 0)
    cols = pl.BlockSpec((d, fc), lambda i, f: (0, f))
    rows = pl.BlockSpec((fc, d), lambda i, f: (f, 0))
    out_specs = [pl.BlockSpec((tm, d), row)]
    out_shape = [jax.ShapeDtypeStruct((m, d), F32)]
    if cast_w:
        out_specs += [cols, cols, rows]
        out_shape += [jax.ShapeDtypeStruct(w.shape, BF16) for w in (wg, wu, wd)]
    out = pl.pallas_call(
        functools.partial(_ffn_kernel, n_chunks=n_chunks, cast_w=cast_w),
        grid=(m // tm, n_chunks),
        in_specs=[pl.BlockSpec((tm, d), row), pl.BlockSpec((1, d), vec), cols, cols, rows,
                  pl.BlockSpec((1, d), vec)],
        out_specs=out_specs,
        out_shape=out_shape,
        scratch_shapes=[pltpu.VMEM((tm, d), BF16)],
        compiler_params=pltpu.CompilerParams(
            dimension_semantics=("arbitrary", "arbitrary"),
            vmem_limit_bytes=VMEM_LIMIT),
        name=name,
    )(x2, g_ffn.reshape(1, d), wg, wu, wd, g_final.reshape(1, d))
    return tuple(out) if cast_w else out[0]


def kernel(x_prompt, x_sample, mem_prompt, state_pool, cache_mem_k, cache_mem_v, g_mix, w_in, b_gate, g_sgu_v, w_sgu, b_sgu, w_pool, pool_scale, g_mem, w_mk, w_mv, w_pa, w_pb, w_pc, w_o, g_ffn, w_ff_gate, w_ff_up, w_ff_down, g_final):
    depth = g_mix.shape[0]
    batch, seq, d = x_prompt.shape
    dec_batch, dec_seq, _ = x_sample.shape
    assert depth == 1, "final norm is fused into the last layer's FFN call"
    assert d == IN_COLS and dec_seq >= POOL_STATE

    xp = x_prompt.reshape(batch * seq, d)
    xs = x_sample.reshape(dec_batch * dec_seq, d)
    mem2 = mem_prompt.reshape(batch * N_MEM, d)

    pool_p, pool_s, mk_p, mv_p, v_s = [], [], [], [], []
    for l in range(depth):
        wpool_b = w_pool[l].astype(BF16)
        wpa_b, wpb_b, wpc_b, wo_b = (w[l].astype(BF16) for w in (w_pa, w_pb, w_pc, w_o))
        branch_weights = (wpool_b, wpa_b, wpb_b, wpc_b)
        gv, ws, bs, pscale = g_sgu_v[l], w_sgu[l], b_sgu[l], pool_scale[l]

        zs, w_in_b = _in_proj(xs, g_mix[l], w_in[l], b_gate[l], True, "in_proj_sample")
        prefix = jnp.pad(state_pool[l], ((0, 0), (dec_seq - POOL_STATE, 0), (0, 0)))
        prefix = prefix.reshape(dec_batch * dec_seq, POOL_WIDTH).astype(BF16)
        ms = _mixer(zs, prefix, False,
                    jnp.transpose(cache_mem_k[l], (0, 2, 3, 1)).reshape(dec_batch, MEM_WIDTH, N_MEM).astype(BF16),
                    cache_mem_v[l].reshape(dec_batch, N_MEM, MEM_WIDTH),
                    gv, ws[:, :dec_seq, :dec_seq], jnp.transpose(bs[:, :dec_seq]), pscale, *branch_weights,
                    n_seq=min(MIXER_SAMPLE_SEQS, dec_batch), seq_rows=dec_seq, chunk_len=dec_seq,
                    pos0=PAST_LEN, tiles_per_seq=1, name="mixer_sample")
        x1s = _out_proj(xs, ms, wo_b, "out_proj_sample")
        xs, wg_b, wu_b, wd_b = _ffn(x1s, g_ffn[l], w_ff_gate[l], w_ff_up[l], w_ff_down[l], g_final,
                                    cast_w=True, name="ffn_sample")
        zs3 = zs.reshape(dec_batch, dec_seq, -1)
        pool_s.append(zs3[:, dec_seq - POOL_STATE:, OFF_XB:OFF_XB + POOL_WIDTH].astype(F32))
        v_s.append(zs3[:, :, SGU_WIDTH:2 * SGU_WIDTH].astype(F32))

        mk, mv, mkt = _mem_kv(mem2, g_mem[l], w_mk[l].astype(BF16), w_mv[l].astype(BF16), "mem_kv")
        mk_p.append(mk.reshape(batch, N_MEM, MEM_HEADS, MEM_HDIM))
        mv_p.append(mv.reshape(batch, N_MEM, MEM_HEADS, MEM_HDIM))

        zp = _in_proj(xp, g_mix[l], w_in_b, b_gate[l], False, "in_proj_prompt")
        tm_p = min(MIXER_ROWS, seq)
        mp = _mixer(zp, zp, True, mkt, mv.reshape(batch, N_MEM, MEM_WIDTH),
                    gv, ws, jnp.transpose(bs), pscale, *branch_weights,
                    n_seq=1, seq_rows=tm_p, chunk_len=SGU_CHUNK, pos0=0, tiles_per_seq=seq // tm_p,
                    name="mixer_prompt")
        x1p = _out_proj(xp, mp, wo_b, "out_proj_prompt")
        xp = _ffn(x1p, g_ffn[l], wg_b, wu_b, wd_b, g_final, cast_w=False, name="ffn_prompt")
        xb_p = zp.reshape(batch, seq, -1)[:, seq - POOL_STATE:, OFF_XB:OFF_XB + POOL_WIDTH]
        pool_p.append(xb_p.astype(F32))

    y_prompt = xp.reshape(batch, seq, d)
    y_sample = xs.reshape(dec_batch, dec_seq, d)
    return (y_prompt, y_sample, jnp.stack(pool_p), jnp.stack(pool_s),
            jnp.stack(mk_p), jnp.stack(mv_p), jnp.stack(v_s))
```

```python
import functools

import jax
import jax.numpy as jnp
from jax import lax
from jax.experimental import pallas as pl
from jax.experimental.pallas import tpu as pltpu

F32 = jnp.float32
BF16 = jnp.bfloat16

EPS = 1e-6
CHUNK_SHIFT = 6
SGU_CHUNK = 128
SGU_WIDTH = 1024
SGU_GROUPS = 4
SGU_GDIM = SGU_WIDTH // SGU_GROUPS
POOL_WIDTH = 1024
POOL_WINDOWS = (2, 4, 8, 16)
POOL_GDIM = POOL_WIDTH // len(POOL_WINDOWS)
POOL_STATE = max(POOL_WINDOWS) - 1
N_MEM = 256
MEM_HEADS = 4
MEM_HDIM = 256
MEM_WIDTH = MEM_HEADS * MEM_HDIM
PAST_LEN = 1024
OFF_XB = 2 * SGU_WIDTH
OFF_GATES = OFF_XB + POOL_WIDTH + MEM_WIDTH

MIB = 1024 * 1024

IN_ROWS = 1024
IN_COLS = 2048
IN_COLS_CAST = 1024
MEM_KV_BATCHES = 2
MIXER_ROWS = 512
MIXER_SAMPLE_SEQS = 4
OUT_ROWS = 1024
OUT_MIN_STEPS = 4
FFN_ROWS = 1024
FFN_COLS = 512
FFN_COLS_CAST = 256
ROW_CHUNK = 512
VMEM_LIMIT = 57 * MIB


def _rms(x, gain):
    ms = jnp.mean(x * x, axis=-1, keepdims=True)
    return (x * lax.rsqrt(ms + EPS)) * gain


def _gelu_tanh(x):
    cdf = 0.5 * (1.0 + jnp.tanh(0.7978845608028654 * (x + 0.044715 * (x * x * x))))
    return x * cdf


def _sigmoid(x):
    return 0.5 + 0.5 * jnp.tanh(0.5 * x)


def _row_chunks(rows):
    step = min(ROW_CHUNK, rows)
    return [slice(r, r + step) for r in range(0, rows, step)]


def _norm_matmul_kernel(*refs, epilogues, has_bias, cast_w):
    refs = list(refs)
    h_ref = refs.pop()
    wb_ref = refs.pop() if cast_w else None
    o_ref = refs.pop()
    b_ref = refs.pop() if has_bias else None
    x_ref, g_ref, w_ref = refs
    n = pl.program_id(1)

    def block(kind, with_norm):
        if cast_w:
            wb_ref[...] = w_ref[...].astype(BF16)
        w_src = wb_ref if cast_w else w_ref
        for rs in _row_chunks(x_ref.shape[0]):
            if with_norm:
                h_ref[rs, :] = _rms(x_ref[rs, :], g_ref[...]).astype(BF16)
            z = jnp.dot(h_ref[rs, :], w_src[...], preferred_element_type=F32)
            if kind == "gelu":
                z = _gelu_tanh(z)
            elif kind == "sigmoid_bias":
                z = _sigmoid((z + b_ref[...]).astype(o_ref.dtype))
            o_ref[rs, :] = z.astype(o_ref.dtype)

    for lo, hi, kind in epilogues:
        if lo == 0:
            pl.when(n == 0)(functools.partial(block, kind, True))
            lo = 1
        if hi > lo:
            pl.when((n >= lo) & (n < hi))(functools.partial(block, kind, False))


def _norm_matmul(x2, gain, w, bias, epilogues, out_dtype, bn, bias_block0=0, cast_w=False, name="norm_matmul"):
    m, k = x2.shape
    n_cols = w.shape[1]
    bm = min(IN_ROWS, m)
    grid = (m // bm, n_cols // bn)
    assert not cast_w or grid[0] == 1, "the bf16 weight copy is written once per column block"
    in_specs = [
        pl.BlockSpec((bm, k), lambda i, j: (i, 0)),
        pl.BlockSpec((1, k), lambda i, j: (0, 0)),
        pl.BlockSpec((k, bn), lambda i, j: (0, j)),
    ]
    args = [x2, gain.reshape(1, k), w]
    if bias is not None:
        in_specs.append(pl.BlockSpec((1, bn), lambda i, j: (0, jnp.maximum(j - bias_block0, 0))))
        args.append(bias.reshape(1, -1))
    out_specs = [pl.BlockSpec((bm, bn), lambda i, j: (i, j))]
    out_shape = [jax.ShapeDtypeStruct((m, n_cols), out_dtype)]
    if cast_w:
        out_specs.append(pl.BlockSpec((k, bn), lambda i, j: (0, j)))
        out_shape.append(jax.ShapeDtypeStruct((k, n_cols), BF16))
    out = pl.pallas_call(
        functools.partial(_norm_matmul_kernel, epilogues=epilogues, has_bias=bias is not None, cast_w=cast_w),
        grid=grid,
        in_specs=in_specs,
        out_specs=out_specs,
        out_shape=out_shape,
        scratch_shapes=[pltpu.VMEM((bm, k), BF16)],
        compiler_params=pltpu.CompilerParams(
            dimension_semantics=("arbitrary", "arbitrary"),
            vmem_limit_bytes=VMEM_LIMIT),
        name=name,
    )(*args)
    return tuple(out) if cast_w else out[0]


def _in_proj(x2, gain, w, bias, cast_w, name):
    bn = IN_COLS_CAST if cast_w else IN_COLS
    epilogues = ((0, OFF_XB // bn, "gelu"), (OFF_XB // bn, OFF_GATES // bn, "identity"),
                 (OFF_GATES // bn, w.shape[1] // bn, "sigmoid_bias"))
    return _norm_matmul(x2, gain, w, bias, epilogues, BF16, bn, bias_block0=OFF_GATES // bn, cast_w=cast_w,
                        name=name)


def _mem_kv_kernel(x_ref, g_ref, wk_ref, wv_ref, k_ref, v_ref, kt_ref):
    h = _rms(x_ref[...], g_ref[...]).astype(BF16)
    k = jnp.dot(h, wk_ref[...], preferred_element_type=F32)
    k_ref[...] = k
    v_ref[...] = jnp.dot(h, wv_ref[...], preferred_element_type=F32)
    for b in range(kt_ref.shape[0]):
        for hd in range(MEM_HEADS):
            lo, hi = hd * MEM_HDIM, (hd + 1) * MEM_HDIM
            kt_ref[b, lo:hi, :] = k[b * N_MEM:(b + 1) * N_MEM, lo:hi].T.astype(BF16)


def _mem_kv(mem2, gain, wk, wv, name):
    m, d = mem2.shape
    bm = MEM_KV_BATCHES * N_MEM
    row = lambda i: (i, 0)
    const = lambda i: (0, 0)
    return pl.pallas_call(
        _mem_kv_kernel,
        grid=(m // bm,),
        in_specs=[pl.BlockSpec((bm, d), row), pl.BlockSpec((1, d), const),
                  pl.BlockSpec((d, MEM_WIDTH), const), pl.BlockSpec((d, MEM_WIDTH), const)],
        out_specs=[pl.BlockSpec((bm, MEM_WIDTH), row), pl.BlockSpec((bm, MEM_WIDTH), row),
                   pl.BlockSpec((MEM_KV_BATCHES, MEM_WIDTH, N_MEM), lambda i: (i, 0, 0))],
        out_shape=[jax.ShapeDtypeStruct((m, MEM_WIDTH), F32), jax.ShapeDtypeStruct((m, MEM_WIDTH), F32),
                   jax.ShapeDtypeStruct((m // N_MEM, MEM_WIDTH, N_MEM), BF16)],
        compiler_params=pltpu.CompilerParams(
            dimension_semantics=("arbitrary",),
            vmem_limit_bytes=VMEM_LIMIT),
        name=name,
    )(mem2, gain.reshape(1, d), wk, wv)


def _mixer_kernel(u_ref, v_ref, xb_ref, q_ref, ga_ref, gb_ref, gc_ref, pre_ref,
                  mk_ref, mv_ref, gv_ref, ws_ref, bs_ref, ps_ref, wpool_ref,
                  wpa_ref, wpb_ref, wpc_ref,
                  o_ref, sgu_scr, pool_scr, att_scr,
                  *, n_seq, seq_rows, chunk_len, pool_chunk, pos0, tiles_per_seq, zero_first_prefix):
    i = pl.program_id(0)
    tile_in_seq = i % tiles_per_seq
    tile_rows = n_seq * seq_rows

    ri = lax.broadcasted_iota(jnp.int32, (chunk_len, chunk_len), 0)
    cj = lax.broadcasted_iota(jnp.int32, (chunk_len, chunk_len), 1)
    causal = jnp.right_shift(cj, CHUNK_SHIFT) <= jnp.right_shift(ri, CHUNK_SHIFT)
    ws = [jnp.where(causal, ws_ref[g], 0.0).astype(BF16) for g in range(SGU_GROUPS)]
    bs_full = [jnp.broadcast_to(bs_ref[:, g:g + 1], (chunk_len, SGU_GDIM)) for g in range(SGU_GROUPS)]

    bt = lax.broadcasted_iota(jnp.int32, (pool_chunk, 2 * pool_chunk), 0)
    bj = lax.broadcasted_iota(jnp.int32, (pool_chunk, 2 * pool_chunk), 1)
    back = bt + pool_chunk - bj
    bands = [jnp.where(back >= 0, jnp.where(back < w, 1.0, 0.0), 0.0).astype(BF16) for w in POOL_WINDOWS]
    trow = lax.broadcasted_iota(jnp.int32, (pool_chunk, 1), 0)

    keep_prefix = jnp.where(tile_in_seq == 0, 0.0, 1.0) if zero_first_prefix else None

    for s in range(n_seq):
        r0 = s * seq_rows

        kb = mk_ref[s].astype(BF16)
        vb = mv_ref[s].astype(BF16)
        for h in range(MEM_HEADS):
            lo, hi = h * MEM_HDIM, (h + 1) * MEM_HDIM
            qh = q_ref[r0:r0 + seq_rows, lo:hi]
            sc = jnp.dot(qh, kb[lo:hi, :], preferred_element_type=F32) * (MEM_HDIM ** -0.5)
            e = jnp.exp(sc - jnp.max(sc, axis=-1, keepdims=True))
            p = (e / jnp.sum(e, axis=-1, keepdims=True)).astype(BF16)
            att_scr[r0:r0 + seq_rows, lo:hi] = jnp.dot(
                p, vb[:, lo:hi], preferred_element_type=F32).astype(BF16)

    assert chunk_len == pool_chunk
    chunk_rows = [(s, c, s * seq_rows + c * chunk_len) for s in range(n_seq) for c in range(seq_rows // chunk_len)]

    vn = [_rms(v_ref[s * seq_rows:(s + 1) * seq_rows, :].astype(F32), gv_ref[...]).astype(BF16)
          for s in range(n_seq)]
    for g in range(SGU_GROUPS):
        lo, hi = g * SGU_GDIM, (g + 1) * SGU_GDIM
        v_side = jnp.concatenate([vn[s][c * chunk_len:(c + 1) * chunk_len, lo:hi] for s, c, _ in chunk_rows], axis=1)
        mixed = jnp.dot(ws[g], v_side, preferred_element_type=F32)
        for k, (_, _, r) in enumerate(chunk_rows):
            u = u_ref[r:r + chunk_len, lo:hi].astype(F32)
            mixed_k = mixed[:, k * SGU_GDIM:(k + 1) * SGU_GDIM] + bs_full[g]
            sgu_scr[r:r + chunk_len, lo:hi] = (u * mixed_k).astype(BF16)

    for g, w in enumerate(POOL_WINDOWS):
        lo, hi = g * POOL_GDIM, (g + 1) * POOL_GDIM
        windows = []
        for s, c, r in chunk_rows:
            if c == 0:
                prev = pre_ref[s * pool_chunk:(s + 1) * pool_chunk, lo:hi]
                if zero_first_prefix:
                    prev = (prev.astype(F32) * keep_prefix).astype(BF16)
            else:
                prev = xb_ref[r - pool_chunk:r, lo:hi]
            windows.append(jnp.concatenate([prev, xb_ref[r:r + pool_chunk, lo:hi]], axis=0))
        wsum = jnp.dot(bands[g], jnp.concatenate(windows, axis=1), preferred_element_type=F32)
        pooled = []
        for k, (_, c, r) in enumerate(chunk_rows):
            pos = pos0 + tile_in_seq * tile_rows + c * pool_chunk + trow
            cnt = jnp.minimum(pos + 1, w).astype(F32)
            cur = xb_ref[r:r + pool_chunk, lo:hi].astype(F32)
            pooled.append((wsum[:, k * POOL_GDIM:(k + 1) * POOL_GDIM] / cnt - cur).astype(BF16))
        pg = jnp.dot(jnp.concatenate(pooled, axis=0), wpool_ref[g], preferred_element_type=F32)
        pool_scr[:, lo:hi] = (pg * ps_ref[:, lo:hi]).astype(BF16)

    merged = ga_ref[...].astype(F32) * jnp.dot(sgu_scr[...], wpa_ref[...], preferred_element_type=F32)
    merged += gb_ref[...].astype(F32) * jnp.dot(pool_scr[...], wpb_ref[...], preferred_element_type=F32)
    merged += gc_ref[...].astype(F32) * jnp.dot(att_scr[...], wpc_ref[...], preferred_element_type=F32)
    o_ref[...] = merged.astype(BF16)


def _mixer(z, prefix, prefix_in_z, mem_k, mem_v, gv, ws, bs_t, pscale, wpool, wpa, wpb, wpc,
           *, n_seq, seq_rows, chunk_len, pos0, tiles_per_seq, name):
    m = z.shape[0]
    d = wpa.shape[1]
    tm = n_seq * seq_rows
    pool_chunk = min(SGU_CHUNK, seq_rows)
    grid = (m // tm,)
    const2 = lambda i: (0, 0)
    const3 = lambda i: (0, 0, 0)
    single = pl.Buffered(1)
    gate_block0 = OFF_GATES // d
    if prefix_in_z:
        blocks_per_tile = tm // pool_chunk
        pre_spec = pl.BlockSpec((pool_chunk, POOL_WIDTH),
                                lambda i: (jnp.maximum(i * blocks_per_tile - 1, 0), OFF_XB // POOL_WIDTH))
        mem_idx = lambda i: (i // tiles_per_seq, 0, 0)
    else:
        pre_spec = pl.BlockSpec((n_seq * pool_chunk, POOL_WIDTH), lambda i: (i, 0))
        mem_idx = lambda i: (i, 0, 0)
    in_specs = [
        pl.BlockSpec((tm, SGU_WIDTH), lambda i: (i, 0)),
        pl.BlockSpec((tm, SGU_WIDTH), lambda i: (i, 1)),
        pl.BlockSpec((tm, POOL_WIDTH), lambda i: (i, 2)),
        pl.BlockSpec((tm, MEM_WIDTH), lambda i: (i, 3)),
        pl.BlockSpec((tm, d), lambda i: (i, gate_block0)),
        pl.BlockSpec((tm, d), lambda i: (i, gate_block0 + 1)),
        pl.BlockSpec((tm, d), lambda i: (i, gate_block0 + 2)),
        pre_spec,
        pl.BlockSpec((n_seq, MEM_WIDTH, N_MEM), mem_idx),
        pl.BlockSpec((n_seq, N_MEM, MEM_WIDTH), mem_idx),
        pl.BlockSpec((1, SGU_WIDTH), const2),
        pl.BlockSpec((SGU_GROUPS, chunk_len, chunk_len), const3),
        pl.BlockSpec((chunk_len, SGU_GROUPS), const2),
        pl.BlockSpec((1, POOL_WIDTH), const2),
        pl.BlockSpec((len(POOL_WINDOWS), POOL_GDIM, POOL_GDIM), const3, pipeline_mode=single),
        pl.BlockSpec((SGU_WIDTH, d), const2, pipeline_mode=single),
        pl.BlockSpec((POOL_WIDTH, d), const2, pipeline_mode=single),
        pl.BlockSpec((MEM_WIDTH, d), const2, pipeline_mode=single),
    ]
    kern = functools.partial(
        _mixer_kernel, n_seq=n_seq, seq_rows=seq_rows, chunk_len=chunk_len, pool_chunk=pool_chunk,
        pos0=pos0, tiles_per_seq=tiles_per_seq, zero_first_prefix=prefix_in_z)
    return pl.pallas_call(
        kern,
        grid=grid,
        in_specs=in_specs,
        out_specs=pl.BlockSpec((tm, d), lambda i: (i, 0)),
        out_shape=jax.ShapeDtypeStruct((m, d), BF16),
        scratch_shapes=[pltpu.VMEM((tm, SGU_WIDTH), BF16),
                        pltpu.VMEM((tm, POOL_WIDTH), BF16),
                        pltpu.VMEM((tm, MEM_WIDTH), BF16)],
        compiler_params=pltpu.CompilerParams(
            dimension_semantics=("arbitrary",),
            vmem_limit_bytes=VMEM_LIMIT),
        name=name,
    )(z, z, z, z, z, z, z, prefix, mem_k, mem_v, gv.reshape(1, -1), ws, bs_t,
      pscale.reshape(1, -1), wpool, wpa, wpb, wpc)


def _out_proj_kernel(x_ref, m_ref, w_ref, o_ref):
    for rs in _row_chunks(x_ref.shape[0]):
        o_ref[rs, :] = x_ref[rs, :] + jnp.dot(m_ref[rs, :], w_ref[...], preferred_element_type=F32)


def _out_proj(x2, merged, wo, name):
    m, d = x2.shape
    tm = min(OUT_ROWS, m // OUT_MIN_STEPS)
    row = lambda i: (i, 0)
    return pl.pallas_call(
        _out_proj_kernel,
        grid=(m // tm,),
        in_specs=[pl.BlockSpec((tm, d), row), pl.BlockSpec((tm, d), row),
                  pl.BlockSpec((d, d), lambda i: (0, 0), pipeline_mode=pl.Buffered(1))],
        out_specs=pl.BlockSpec((tm, d), row),
        out_shape=jax.ShapeDtypeStruct((m, d), F32),
        compiler_params=pltpu.CompilerParams(
            dimension_semantics=("arbitrary",),
            vmem_limit_bytes=VMEM_LIMIT),
        name=name,
    )(x2, merged, wo)


def _ffn_kernel(*refs, n_chunks, cast_w):
    refs = list(refs)
    h_ref = refs.pop()
    if cast_w:
        wdb_ref, wub_ref, wgb_ref = refs.pop(), refs.pop(), refs.pop()
    o_ref = refs.pop()
    x_ref, g_ref, wg_ref, wu_ref, wd_ref, gf_ref = refs
    f = pl.program_id(1)

    if cast_w:
        wgb_ref[...] = wg_ref[...].astype(BF16)
        wub_ref[...] = wu_ref[...].astype(BF16)
        wdb_ref[...] = wd_ref[...].astype(BF16)
        wg_ref, wu_ref, wd_ref = wgb_ref, wub_ref, wdb_ref

    def step(first, last):
        for rs in _row_chunks(x_ref.shape[0]):
            if first:
                h_ref[rs, :] = _rms(x_ref[rs, :], g_ref[...]).astype(BF16)
            h = h_ref[rs, :]
            gate = jnp.dot(h, wg_ref[...], preferred_element_type=F32)
            up = jnp.dot(h, wu_ref[...], preferred_element_type=F32)
            act = ((gate * _sigmoid(gate)) * up).astype(BF16)
            acc = x_ref[rs, :] if first else o_ref[rs, :]
            y = acc + jnp.dot(act, wd_ref[...], preferred_element_type=F32)
            o_ref[rs, :] = _rms(y, gf_ref[...]) if last else y

    pl.when(f == 0)(functools.partial(step, True, False))
    pl.when((f > 0) & (f < n_chunks - 1))(functools.partial(step, False, False))
    pl.when(f == n_chunks - 1)(functools.partial(step, False, True))


def _ffn(x2, g_ffn, wg, wu, wd, g_final, *, cast_w, name):
    m, d = x2.shape
    d_ff = wg.shape[1]
    tm = min(FFN_ROWS, m)
    assert not cast_w or m == tm, "the bf16 weight copies are written once per d_ff chunk"
    fc = FFN_COLS_CAST if cast_w else FFN_COLS
    n_chunks = d_ff // fc
    row = lambda i, f: (i, 0)
    vec = lambda i, f: (0, 0)
    cols = pl.BlockSpec((d, fc), lambda i, f: (0, f))
    rows = pl.BlockSpec((fc, d), lambda i, f: (f, 0))
    out_specs = [pl.BlockSpec((tm, d), row)]
    out_shape = [jax.ShapeDtypeStruct((m, d), F32)]
    if cast_w:
        out_specs += [cols, cols, rows]
        out_shape += [jax.ShapeDtypeStruct(w.shape, BF16) for w in (wg, wu, wd)]
    out = pl.pallas_call(
        functools.partial(_ffn_kernel, n_chunks=n_chunks, cast_w=cast_w),
        grid=(m // tm, n_chunks),
        in_specs=[pl.BlockSpec((tm, d), row), pl.BlockSpec((1, d), vec), cols, cols, rows,
                  pl.BlockSpec((1, d), vec)],
        out_specs=out_specs,
        out_shape=out_shape,
        scratch_shapes=[pltpu.VMEM((tm, d), BF16)],
        compiler_params=pltpu.CompilerParams(
            dimension_semantics=("arbitrary", "arbitrary"),
            vmem_limit_bytes=VMEM_LIMIT),
        name=name,
    )(x2, g_ffn.reshape(1, d), wg, wu, wd, g_final.reshape(1, d))
    return tuple(out) if cast_w else out[0]


def kernel(x_prompt, x_sample, mem_prompt, state_pool, cache_mem_k, cache_mem_v, g_mix, w_in, b_gate, g_sgu_v, w_sgu, b_sgu, w_pool, pool_scale, g_mem, w_mk, w_mv, w_pa, w_pb, w_pc, w_o, g_ffn, w_ff_gate, w_ff_up, w_ff_down, g_final):
    depth = g_mix.shape[0]
    batch, seq, d = x_prompt.shape
    dec_batch, dec_seq, _ = x_sample.shape
    assert depth == 1, "final norm is fused into the last layer's FFN call"
    assert d == IN_COLS and dec_seq >= POOL_STATE

    xp = x_prompt.reshape(batch * seq, d)
    xs = x_sample.reshape(dec_batch * dec_seq, d)
    mem2 = mem_prompt.reshape(batch * N_MEM, d)

    pool_p, pool_s, mk_p, mv_p, v_s = [], [], [], [], []
    for l in range(depth):
        wpool_b = w_pool[l].astype(BF16)
        wpa_b, wpb_b, wpc_b, wo_b = (w[l].astype(BF16) for w in (w_pa, w_pb, w_pc, w_o))
        branch_weights = (wpool_b, wpa_b, wpb_b, wpc_b)
        gv, ws, bs, pscale = g_sgu_v[l], w_sgu[l], b_sgu[l], pool_scale[l]

        zs, w_in_b = _in_proj(xs, g_mix[l], w_in[l], b_gate[l], True, "in_proj_sample")
        prefix = jnp.pad(state_pool[l], ((0, 0), (dec_seq - POOL_STATE, 0), (0, 0)))
        prefix = prefix.reshape(dec_batch * dec_seq, POOL_WIDTH).astype(BF16)
        ms = _mixer(zs, prefix, False,
                    jnp.transpose(cache_mem_k[l], (0, 2, 3, 1)).reshape(dec_batch, MEM_WIDTH, N_MEM).astype(BF16),
                    cache_mem_v[l].reshape(dec_batch, N_MEM, MEM_WIDTH),
                    gv, ws[:, :dec_seq, :dec_seq], jnp.transpose(bs[:, :dec_seq]), pscale, *branch_weights,
                    n_seq=min(MIXER_SAMPLE_SEQS, dec_batch), seq_rows=dec_seq, chunk_len=dec_seq,
                    pos0=PAST_LEN, tiles_per_seq=1, name="mixer_sample")
        x1s = _out_proj(xs, ms, wo_b, "out_proj_sample")
        xs, wg_b, wu_b, wd_b = _ffn(x1s, g_ffn[l], w_ff_gate[l], w_ff_up[l], w_ff_down[l], g_final,
                                    cast_w=True, name="ffn_sample")
        zs3 = zs.reshape(dec_batch, dec_seq, -1)
        pool_s.append(zs3[:, dec_seq - POOL_STATE:, OFF_XB:OFF_XB + POOL_WIDTH].astype(F32))
        v_s.append(zs3[:, :, SGU_WIDTH:2 * SGU_WIDTH].astype(F32))

        mk, mv, mkt = _mem_kv(mem2, g_mem[l], w_mk[l].astype(BF16), w_mv[l].astype(BF16), "mem_kv")
        mk_p.append(mk.reshape(batch, N_MEM, MEM_HEADS, MEM_HDIM))
        mv_p.append(mv.reshape(batch, N_MEM, MEM_HEADS, MEM_HDIM))

        zp = _in_proj(xp, g_mix[l], w_in_b, b_gate[l], False, "in_proj_prompt")
        tm_p = min(MIXER_ROWS, seq)
        mp = _mixer(zp, zp, True, mkt, mv.reshape(batch, N_MEM, MEM_WIDTH),
                    gv, ws, jnp.transpose(bs), pscale, *branch_weights,
                    n_seq=1, seq_rows=tm_p, chunk_len=SGU_CHUNK, pos0=0, tiles_per_seq=seq // tm_p,
                    name="mixer_prompt")
        x1p = _out_proj(xp, mp, wo_b, "out_proj_prompt")
        xp = _ffn(x1p, g_ffn[l], wg_b, wu_b, wd_b, g_final, cast_w=False, name="ffn_prompt")
        xb_p = zp.reshape(batch, seq, -1)[:, seq - POOL_STATE:, OFF_XB:OFF_XB + POOL_WIDTH]
        pool_p.append(xb_p.astype(F32))

    y_prompt = xp.reshape(batch, seq, d)
    y_sample = xs.reshape(dec_batch, dec_seq, d)
    return (y_prompt, y_sample, jnp.stack(pool_p), jnp.stack(pool_s),
            jnp.stack(mk_p), jnp.stack(mv_p), jnp.stack(v_s))
```

```python
import functools

import jax
import jax.numpy as jnp
from jax import lax
from jax.experimental import pallas as pl
from jax.experimental.pallas import tpu as pltpu

F32 = jnp.float32
BF16 = jnp.bfloat16

EPS = 1e-6
CHUNK_SHIFT = 6
SGU_CHUNK = 128
SGU_WIDTH = 1024
SGU_GROUPS = 4
SGU_GDIM = SGU_WIDTH // SGU_GROUPS
POOL_WIDTH = 1024
POOL_WINDOWS = (2, 4, 8, 16)
POOL_GDIM = POOL_WIDTH // len(POOL_WINDOWS)
POOL_STATE = max(POOL_WINDOWS) - 1
N_MEM = 256
MEM_HEADS = 4
MEM_HDIM = 256
MEM_WIDTH = MEM_HEADS * MEM_HDIM
PAST_LEN = 1024
OFF_XB = 2 * SGU_WIDTH
OFF_GATES = OFF_XB + POOL_WIDTH + MEM_WIDTH

MIB = 1024 * 1024

IN_ROWS = 1024
IN_COLS = 2048
IN_COLS_CAST = 1024
MEM_KV_BATCHES = 2
MIXER_ROWS = 512
MIXER_SAMPLE_SEQS = 4
OUT_ROWS = 1024
OUT_MIN_STEPS = 4
FFN_ROWS = 1024
FFN_COLS = 512
FFN_COLS_CAST = 256
ROW_CHUNK = 512
VMEM_LIMIT = 57 * MIB


def _rms(x, gain):
    ms = jnp.mean(x * x, axis=-1, keepdims=True)
    return (x * lax.rsqrt(ms + EPS)) * gain


def _gelu_tanh(x):
    cdf = 0.5 * (1.0 + jnp.tanh(0.7978845608028654 * (x + 0.044715 * (x * x * x))))
    return x * cdf


def _sigmoid(x):
    return 0.5 + 0.5 * jnp.tanh(0.5 * x)


def _row_chunks(rows):
    step = min(ROW_CHUNK, rows)
    return [slice(r, r + step) for r in range(0, rows, step)]


def _norm_matmul_kernel(*refs, epilogues, has_bias, cast_w):
    refs = list(refs)
    h_ref = refs.pop()
    wb_ref = refs.pop() if cast_w else None
    o_ref = refs.pop()
    b_ref = refs.pop() if has_bias else None
    x_ref, g_ref, w_ref = refs
    n = pl.program_id(1)

    def block(kind, with_norm):
        if cast_w:
            wb_ref[...] = w_ref[...].astype(BF16)
        w_src = wb_ref if cast_w else w_ref
        for rs in _row_chunks(x_ref.shape[0]):
            if with_norm:
                h_ref[rs, :] = _rms(x_ref[rs, :], g_ref[...]).astype(BF16)
            z = jnp.dot(h_ref[rs, :], w_src[...], preferred_element_type=F32)
            if kind == "gelu":
                z = _gelu_tanh(z)
            elif kind == "sigmoid_bias":
                z = _sigmoid((z + b_ref[...]).astype(o_ref.dtype))
            o_ref[rs, :] = z.astype(o_ref.dtype)

    for lo, hi, kind in epilogues:
        if lo == 0:
            pl.when(n == 0)(functools.partial(block, kind, True))
            lo = 1
        if hi > lo:
            pl.when((n >= lo) & (n < hi))(functools.partial(block, kind, False))


def _norm_matmul(x2, gain, w, bias, epilogues, out_dtype, bn, bias_block0=0, cast_w=False, name="norm_matmul"):
    m, k = x2.shape
    n_cols = w.shape[1]
    bm = min(IN_ROWS, m)
    grid = (m // bm, n_cols // bn)
    assert not cast_w or grid[0] == 1, "the bf16 weight copy is written once per column block"
    in_specs = [
        pl.BlockSpec((bm, k), lambda i, j: (i, 0)),
        pl.BlockSpec((1, k), lambda i, j: (0, 0)),
        pl.BlockSpec((k, bn), lambda i, j: (0, j)),
    ]
    args = [x2, gain.reshape(1, k), w]
    if bias is not None:
        in_specs.append(pl.BlockSpec((1, bn), lambda i, j: (0, jnp.maximum(j - bias_block0, 0))))
        args.append(bias.reshape(1, -1))
    out_specs = [pl.BlockSpec((bm, bn), lambda i, j: (i, j))]
    out_shape = [jax.ShapeDtypeStruct((m, n_cols), out_dtype)]
    if cast_w:
        out_specs.append(pl.BlockSpec((k, bn), lambda i, j: (0, j)))
        out_shape.append(jax.ShapeDtypeStruct((k, n_cols), BF16))
    out = pl.pallas_call(
        functools.partial(_norm_matmul_kernel, epilogues=epilogues, has_bias=bias is not None, cast_w=cast_w),
        grid=grid,
        in_specs=in_specs,
        out_specs=out_specs,
        out_shape=out_shape,
        scratch_shapes=[pltpu.VMEM((bm, k), BF16)],
        compiler_params=pltpu.CompilerParams(
            dimension_semantics=("arbitrary", "arbitrary"),
            vmem_limit_bytes=VMEM_LIMIT),
        name=name,
    )(*args)
    return tuple(out) if cast_w else out[0]


def _in_proj(x2, gain, w, bias, cast_w, name):
    bn = IN_COLS_CAST if cast_w else IN_COLS
    epilogues = ((0, OFF_XB // bn, "gelu"), (OFF_XB // bn, OFF_GATES // bn, "identity"),
                 (OFF_GATES // bn, w.shape[1] // bn, "sigmoid_bias"))
    return _norm_matmul(x2, gain, w, bias, epilogues, BF16, bn, bias_block0=OFF_GATES // bn, cast_w=cast_w,
                        name=name)


def _mem_kv_kernel(x_ref, g_ref, wk_ref, wv_ref, k_ref, v_ref, kt_ref):
    h = _rms(x_ref[...], g_ref[...]).astype(BF16)
    k = jnp.dot(h, wk_ref[...], preferred_element_type=F32)
    k_ref[...] = k
    v_ref[...] = jnp.dot(h, wv_ref[...], preferred_element_type=F32)
    for b in range(kt_ref.shape[0]):
        for hd in range(MEM_HEADS):
            lo, hi = hd * MEM_HDIM, (hd + 1) * MEM_HDIM
            kt_ref[b, lo:hi, :] = k[b * N_MEM:(b + 1) * N_MEM, lo:hi].T.astype(BF16)


def _mem_kv(mem2, gain, wk, wv, name):
    m, d = mem2.shape
    bm = MEM_KV_BATCHES * N_MEM
    row = lambda i: (i, 0)
    const = lambda i: (0, 0)
    return pl.pallas_call(
        _mem_kv_kernel,
        grid=(m // bm,),
        in_specs=[pl.BlockSpec((bm, d), row), pl.BlockSpec((1, d), const),
                  pl.BlockSpec((d, MEM_WIDTH), const), pl.BlockSpec((d, MEM_WIDTH), const)],
        out_specs=[pl.BlockSpec((bm, MEM_WIDTH), row), pl.BlockSpec((bm, MEM_WIDTH), row),
                   pl.BlockSpec((MEM_KV_BATCHES, MEM_WIDTH, N_MEM), lambda i: (i, 0, 0))],
        out_shape=[jax.ShapeDtypeStruct((m, MEM_WIDTH), F32), jax.ShapeDtypeStruct((m, MEM_WIDTH), F32),
                   jax.ShapeDtypeStruct((m // N_MEM, MEM_WIDTH, N_MEM), BF16)],
        compiler_params=pltpu.CompilerParams(
            dimension_semantics=("arbitrary",),
            vmem_limit_bytes=VMEM_LIMIT),
        name=name,
    )(mem2, gain.reshape(1, d), wk, wv)


def _mixer_kernel(u_ref, v_ref, xb_ref, q_ref, ga_ref, gb_ref, gc_ref, pre_ref,
                  mk_ref, mv_ref, gv_ref, ws_ref, bs_ref, ps_ref, wpool_ref,
                  wpa_ref, wpb_ref, wpc_ref,
                  o_ref, sgu_scr, pool_scr, att_scr,
                  *, n_seq, seq_rows, chunk_len, pool_chunk, pos0, tiles_per_seq, zero_first_prefix):
    i = pl.program_id(0)
    tile_in_seq = i % tiles_per_seq
    tile_rows = n_seq * seq_rows

    ri = lax.broadcasted_iota(jnp.int32, (chunk_len, chunk_len), 0)
    cj = lax.broadcasted_iota(jnp.int32, (chunk_len, chunk_len), 1)
    causal = jnp.right_shift(cj, CHUNK_SHIFT) <= jnp.right_shift(ri, CHUNK_SHIFT)
    ws = [jnp.where(causal, ws_ref[g], 0.0).astype(BF16) for g in range(SGU_GROUPS)]
    bs_full = [jnp.broadcast_to(bs_ref[:, g:g + 1], (chunk_len, SGU_GDIM)) for g in range(SGU_GROUPS)]

    bt = lax.broadcasted_iota(jnp.int32, (pool_chunk, 2 * pool_chunk), 0)
    bj = lax.broadcasted_iota(jnp.int32, (pool_chunk, 2 * pool_chunk), 1)
    back = bt + pool_chunk - bj
    bands = [jnp.where(back >= 0, jnp.where(back < w, 1.0, 0.0), 0.0).astype(BF16) for w in POOL_WINDOWS]
    trow = lax.broadcasted_iota(jnp.int32, (pool_chunk, 1), 0)

    keep_prefix = jnp.where(tile_in_seq == 0, 0.0, 1.0) if zero_first_prefix else None

    for s in range(n_seq):
        r0 = s * seq_rows

        kb = mk_ref[s].astype(BF16)
        vb = mv_ref[s].astype(BF16)
        for h in range(MEM_HEADS):
            lo, hi = h * MEM_HDIM, (h + 1) * MEM_HDIM
            qh = q_ref[r0:r0 + seq_rows, lo:hi]
            sc = jnp.dot(qh, kb[lo:hi, :], preferred_element_type=F32) * (MEM_HDIM ** -0.5)
            e = jnp.exp(sc - jnp.max(sc, axis=-1, keepdims=True))
            p = (e / jnp.sum(e, axis=-1, keepdims=True)).astype(BF16)
            att_scr[r0:r0 + seq_rows, lo:hi] = jnp.dot(
                p, vb[:, lo:hi], preferred_element_type=F32).astype(BF16)

    assert chunk_len == pool_chunk
    chunk_rows = [(s, c, s * seq_rows + c * chunk_len) for s in range(n_seq) for c in range(seq_rows // chunk_len)]

    vn = [_rms(v_ref[s * seq_rows:(s + 1) * seq_rows, :].astype(F32), gv_ref[...]).astype(BF16)
          for s in range(n_seq)]
    for g in range(SGU_GROUPS):
        lo, hi = g * SGU_GDIM, (g + 1) * SGU_GDIM
        v_side = jnp.concatenate([vn[s][c * chunk_len:(c + 1) * chunk_len, lo:hi] for s, c, _ in chunk_rows], axis=1)
        mixed = jnp.dot(ws[g], v_side, preferred_element_type=F32)
        for k, (_, _, r) in enumerate(chunk_rows):
            u = u_ref[r:r + chunk_len, lo:hi].astype(F32)
            mixed_k = mixed[:, k * SGU_GDIM:(k + 1) * SGU_GDIM] + bs_full[g]
            sgu_scr[r:r + chunk_len, lo:hi] = (u * mixed_k).astype(BF16)

    for g, w in enumerate(POOL_WINDOWS):
        lo, hi = g * POOL_GDIM, (g + 1) * POOL_GDIM
        windows = []
        for s, c, r in chunk_rows:
            if c == 0:
                prev = pre_ref[s * pool_chunk:(s + 1) * pool_chunk, lo:hi]
                if zero_first_prefix:
                    prev = (prev.astype(F32) * keep_prefix).astype(BF16)
            else:
                prev = xb_ref[r - pool_chunk:r, lo:hi]
            windows.append(jnp.concatenate([prev, xb_ref[r:r + pool_chunk, lo:hi]], axis=0))
        wsum = jnp.dot(bands[g], jnp.concatenate(windows, axis=1), preferred_element_type=F32)
        pooled = []
        for k, (_, c, r) in enumerate(chunk_rows):
            pos = pos0 + tile_in_seq * tile_rows + c * pool_chunk + trow
            cnt = jnp.minimum(pos + 1, w).astype(F32)
            cur = xb_ref[r:r + pool_chunk, lo:hi].astype(F32)
            pooled.append((wsum[:, k * POOL_GDIM:(k + 1) * POOL_GDIM] / cnt - cur).astype(BF16))
        pg = jnp.dot(jnp.concatenate(pooled, axis=0), wpool_ref[g], preferred_element_type=F32)
        pool_scr[:, lo:hi] = (pg * ps_ref[:, lo:hi]).astype(BF16)

    merged = ga_ref[...].astype(F32) * jnp.dot(sgu_scr[...], wpa_ref[...], preferred_element_type=F32)
    merged += gb_ref[...].astype(F32) * jnp.dot(pool_scr[...], wpb_ref[...], preferred_element_type=F32)
    merged += gc_ref[...].astype(F32) * jnp.dot(att_scr[...], wpc_ref[...], preferred_element_type=F32)
    o_ref[...] = merged.astype(BF16)


def _mixer(z, prefix, prefix_in_z, mem_k, mem_v, gv, ws, bs_t, pscale, wpool, wpa, wpb, wpc,
           *, n_seq, seq_rows, chunk_len, pos0, tiles_per_seq, name):
    m = z.shape[0]
    d = wpa.shape[1]
    tm = n_seq * seq_rows
    pool_chunk = min(SGU_CHUNK, seq_rows)
    grid = (m // tm,)
    const2 = lambda i: (0, 0)
    const3 = lambda i: (0, 0, 0)
    single = pl.Buffered(1)
    gate_block0 = OFF_GATES // d
    if prefix_in_z:
        blocks_per_tile = tm // pool_chunk
        pre_spec = pl.BlockSpec((pool_chunk, POOL_WIDTH),
                                lambda i: (jnp.maximum(i * blocks_per_tile - 1, 0), OFF_XB // POOL_WIDTH))
        mem_idx = lambda i: (i // tiles_per_seq, 0, 0)
    else:
        pre_spec = pl.BlockSpec((n_seq * pool_chunk, POOL_WIDTH), lambda i: (i, 0))
        mem_idx = lambda i: (i, 0, 0)
    in_specs = [
        pl.BlockSpec((tm, SGU_WIDTH), lambda i: (i, 0)),
        pl.BlockSpec((tm, SGU_WIDTH), lambda i: (i, 1)),
        pl.BlockSpec((tm, POOL_WIDTH), lambda i: (i, 2)),
        pl.BlockSpec((tm, MEM_WIDTH), lambda i: (i, 3)),
        pl.BlockSpec((tm, d), lambda i: (i, gate_block0)),
        pl.BlockSpec((tm, d), lambda i: (i, gate_block0 + 1)),
        pl.BlockSpec((tm, d), lambda i: (i, gate_block0 + 2)),
        pre_spec,
        pl.BlockSpec((n_seq, MEM_WIDTH, N_MEM), mem_idx),
        pl.BlockSpec((n_seq, N_MEM, MEM_WIDTH), mem_idx),
        pl.BlockSpec((1, SGU_WIDTH), const2),
        pl.BlockSpec((SGU_GROUPS, chunk_len, chunk_len), const3),
        pl.BlockSpec((chunk_len, SGU_GROUPS), const2),
        pl.BlockSpec((1, POOL_WIDTH), const2),
        pl.BlockSpec((len(POOL_WINDOWS), POOL_GDIM, POOL_GDIM), const3, pipeline_mode=single),
        pl.BlockSpec((SGU_WIDTH, d), const2, pipeline_mode=single),
        pl.BlockSpec((POOL_WIDTH, d), const2, pipeline_mode=single),
        pl.BlockSpec((MEM_WIDTH, d), const2, pipeline_mode=single),
    ]
    kern = functools.partial(
        _mixer_kernel, n_seq=n_seq, seq_rows=seq_rows, chunk_len=chunk_len, pool_chunk=pool_chunk,
        pos0=pos0, tiles_per_seq=tiles_per_seq, zero_first_prefix=prefix_in_z)
    return pl.pallas_call(
        kern,
        grid=grid,
        in_specs=in_specs,
        out_specs=pl.BlockSpec((tm, d), lambda i: (i, 0)),
        out_shape=jax.ShapeDtypeStruct((m, d), BF16),
        scratch_shapes=[pltpu.VMEM((tm, SGU_WIDTH), BF16),
                        pltpu.VMEM((tm, POOL_WIDTH), BF16),
                        pltpu.VMEM((tm, MEM_WIDTH), BF16)],
        compiler_params=pltpu.CompilerParams(
            dimension_semantics=("arbitrary",),
            vmem_limit_bytes=VMEM_LIMIT),
        name=name,
    )(z, z, z, z, z, z, z, prefix, mem_k, mem_v, gv.reshape(1, -1), ws, bs_t,
      pscale.reshape(1, -1), wpool, wpa, wpb, wpc)


def _out_proj_kernel(x_ref, m_ref, w_ref, o_ref):
    for rs in _row_chunks(x_ref.shape[0]):
        o_ref[rs, :] = x_ref[rs, :] + jnp.dot(m_ref[rs, :], w_ref[...], preferred_element_type=F32)


def _out_proj(x2, merged, wo, name):
    m, d = x2.shape
    tm = min(OUT_ROWS, m // OUT_MIN_STEPS)
    row = lambda i: (i, 0)
    return pl.pallas_call(
        _out_proj_kernel,
        grid=(m // tm,),
        in_specs=[pl.BlockSpec((tm, d), row), pl.BlockSpec((tm, d), row),
                  pl.BlockSpec((d, d), lambda i: (0, 0), pipeline_mode=pl.Buffered(1))],
        out_specs=pl.BlockSpec((tm, d), row),
        out_shape=jax.ShapeDtypeStruct((m, d), F32),
        compiler_params=pltpu.CompilerParams(
            dimension_semantics=("arbitrary",),
            vmem_limit_bytes=VMEM_LIMIT),
        name=name,
    )(x2, merged, wo)


def _ffn_kernel(*refs, n_chunks, cast_w):
    refs = list(refs)
    h_ref = refs.pop()
    if cast_w:
        wdb_ref, wub_ref, wgb_ref = refs.pop(), refs.pop(), refs.pop()
    o_ref = refs.pop()
    x_ref, g_ref, wg_ref, wu_ref, wd_ref, gf_ref = refs
    f = pl.program_id(1)

    if cast_w:
        wgb_ref[...] = wg_ref[...].astype(BF16)
        wub_ref[...] = wu_ref[...].astype(BF16)
        wdb_ref[...] = wd_ref[...].astype(BF16)
        wg_ref, wu_ref, wd_ref = wgb_ref, wub_ref, wdb_ref

    def step(first, last):
        for rs in _row_chunks(x_ref.shape[0]):
            if first:
                h_ref[rs, :] = _rms(x_ref[rs, :], g_ref[...]).astype(BF16)
            h = h_ref[rs, :]
            gate = jnp.dot(h, wg_ref[...], preferred_element_type=F32)
            up = jnp.dot(h, wu_ref[...], preferred_element_type=F32)
            gate_b = gate.astype(BF16)
            act = (gate_b * _sigmoid(gate_b)) * up.astype(BF16)
            acc = x_ref[rs, :] if first else o_ref[rs, :]
            y = acc + jnp.dot(act, wd_ref[...], preferred_element_type=F32)
            o_ref[rs, :] = _rms(y, gf_ref[...]) if last else y

    pl.when(f == 0)(functools.partial(step, True, False))
    pl.when((f > 0) & (f < n_chunks - 1))(functools.partial(step, False, False))
    pl.when(f == n_chunks - 1)(functools.partial(step, False, True))


def _ffn(x2, g_ffn, wg, wu, wd, g_final, *, cast_w, name):
    m, d = x2.shape
    d_ff = wg.shape[1]
    tm = min(FFN_ROWS, m)
    assert not cast_w or m == tm, "the bf16 weight copies are written once per d_ff chunk"
    fc = FFN_COLS_CAST if cast_w else FFN_COLS
    n_chunks = d_ff // fc
    row = lambda i, f: (i, 0)
    vec = lambda i, f: (0, 0)
    cols = pl.BlockSpec((d, fc), lambda i, f: (0, f))
    rows = pl.BlockSpec((fc, d), lambda i, f: (f, 0))
    out_specs = [pl.BlockSpec((tm, d), row)]
    out_shape = [jax.ShapeDtypeStruct((m, d), F32)]
    if cast_w:
        out_specs += [cols, cols, rows]
        out_shape += [jax.ShapeDtypeStruct(w.shape, BF16) for w in (wg, wu, wd)]
    out = pl.pallas_call(
        functools.partial(_ffn_kernel, n_chunks=n_chunks, cast_w=cast_w),
        grid=(m // tm, n_chunks),
        in_specs=[pl.BlockSpec((tm, d), row), pl.BlockSpec((1, d), vec), cols, cols, rows,
                  pl.BlockSpec((1, d), vec)],
        out_specs=out_specs,
        out_shape=out_shape,
        scratch_shapes=[pltpu.VMEM((tm, d), BF16)],
        compiler_params=pltpu.CompilerParams(
            dimension_semantics=("arbitrary", "arbitrary"),
            vmem_limit_bytes=VMEM_LIMIT),
        name=name,
    )(x2, g_ffn.reshape(1, d), wg, wu, wd, g_final.reshape(1, d))
    return tuple(out) if cast_w else out[0]


def kernel(x_prompt, x_sample, mem_prompt, state_pool, cache_mem_k, cache_mem_v, g_mix, w_in, b_gate, g_sgu_v, w_sgu, b_sgu, w_pool, pool_scale, g_mem, w_mk, w_mv, w_pa, w_pb, w_pc, w_o, g_ffn, w_ff_gate, w_ff_up, w_ff_down, g_final):
    depth = g_mix.shape[0]
    batch, seq, d = x_prompt.shape
    dec_batch, dec_seq, _ = x_sample.shape
    assert depth == 1, "final norm is fused into the last layer's FFN call"
    assert d == IN_COLS and dec_seq >= POOL_STATE

    xp = x_prompt.reshape(batch * seq, d)
    xs = x_sample.reshape(dec_batch * dec_seq, d)
    mem2 = mem_prompt.reshape(batch * N_MEM, d)

    pool_p, pool_s, mk_p, mv_p, v_s = [], [], [], [], []
    for l in range(depth):
        wpool_b = w_pool[l].astype(BF16)
        wpa_b, wpb_b, wpc_b, wo_b = (w[l].astype(BF16) for w in (w_pa, w_pb, w_pc, w_o))
        branch_weights = (wpool_b, wpa_b, wpb_b, wpc_b)
        gv, ws, bs, pscale = g_sgu_v[l], w_sgu[l], b_sgu[l], pool_scale[l]

        zs, w_in_b = _in_proj(xs, g_mix[l], w_in[l], b_gate[l], True, "in_proj_sample")
        prefix = jnp.pad(state_pool[l], ((0, 0), (dec_seq - POOL_STATE, 0), (0, 0)))
        prefix = prefix.reshape(dec_batch * dec_seq, POOL_WIDTH).astype(BF16)
        ms = _mixer(zs, prefix, False,
                    jnp.transpose(cache_mem_k[l], (0, 2, 3, 1)).reshape(dec_batch, MEM_WIDTH, N_MEM).astype(BF16),
                    cache_mem_v[l].reshape(dec_batch, N_MEM, MEM_WIDTH),
                    gv, ws[:, :dec_seq, :dec_seq], jnp.transpose(bs[:, :dec_seq]), pscale, *branch_weights,
                    n_seq=min(MIXER_SAMPLE_SEQS, dec_batch), seq_rows=dec_seq, chunk_len=dec_seq,
                    pos0=PAST_LEN, tiles_per_seq=1, name="mixer_sample")
        x1s = _out_proj(xs, ms, wo_b, "out_proj_sample")
        xs, wg_b, wu_b, wd_b = _ffn(x1s, g_ffn[l], w_ff_gate[l], w_ff_up[l], w_ff_down[l], g_final,
                                    cast_w=True, name="ffn_sample")
        zs3 = zs.reshape(dec_batch, dec_seq, -1)
        pool_s.append(zs3[:, dec_seq - POOL_STATE:, OFF_XB:OFF_XB + POOL_WIDTH].astype(F32))
        v_s.append(zs3[:, :, SGU_WIDTH:2 * SGU_WIDTH].astype(F32))

        mk, mv, mkt = _mem_kv(mem2, g_mem[l], w_mk[l].astype(BF16), w_mv[l].astype(BF16), "mem_kv")
        mk_p.append(mk.reshape(batch, N_MEM, MEM_HEADS, MEM_HDIM))
        mv_p.append(mv.reshape(batch, N_MEM, MEM_HEADS, MEM_HDIM))

        zp = _in_proj(xp, g_mix[l], w_in_b, b_gate[l], False, "in_proj_prompt")
        tm_p = min(MIXER_ROWS, seq)
        mp = _mixer(zp, zp, True, mkt, mv.reshape(batch, N_MEM, MEM_WIDTH),
                    gv, ws, jnp.transpose(bs), pscale, *branch_weights,
                    n_seq=1, seq_rows=tm_p, chunk_len=SGU_CHUNK, pos0=0, tiles_per_seq=seq // tm_p,
                    name="mixer_prompt")
        x1p = _out_proj(xp, mp, wo_b, "out_proj_prompt")
        xp = _ffn(x1p, g_ffn[l], wg_b, wu_b, wd_b, g_final, cast_w=False, name="ffn_prompt")
        xb_p = zp.reshape(batch, seq, -1)[:, seq - POOL_STATE:, OFF_XB:OFF_XB + POOL_WIDTH]
        pool_p.append(xb_p.astype(F32))

    y_prompt = xp.reshape(batch, seq, d)
    y_sample = xs.reshape(dec_batch, dec_seq, d)
    return (y_prompt, y_sample, jnp.stack(pool_p), jnp.stack(pool_s),
            jnp.stack(mk_p), jnp.stack(mv_p), jnp.stack(v_s))
```

```python
import functools

import jax
import jax.numpy as jnp
from jax import lax
from jax.experimental import pallas as pl
from jax.experimental.pallas import tpu as pltpu

F32 = jnp.float32
BF16 = jnp.bfloat16

EPS = 1e-6
CHUNK_SHIFT = 6
SGU_CHUNK = 128
SGU_WIDTH = 1024
SGU_GROUPS = 4
SGU_GDIM = SGU_WIDTH // SGU_GROUPS
POOL_WIDTH = 1024
POOL_WINDOWS = (2, 4, 8, 16)
POOL_GDIM = POOL_WIDTH // len(POOL_WINDOWS)
POOL_STATE = max(POOL_WINDOWS) - 1
N_MEM = 256
MEM_HEADS = 4
MEM_HDIM = 256
MEM_WIDTH = MEM_HEADS * MEM_HDIM
PAST_LEN = 1024
OFF_XB = 2 * SGU_WIDTH
OFF_GATES = OFF_XB + POOL_WIDTH + MEM_WIDTH

MIB = 1024 * 1024

IN_ROWS = 1024
IN_COLS = 2048
IN_COLS_CAST = 1024
MEM_KV_BATCHES = 2
MIXER_ROWS = 512
MIXER_SAMPLE_SEQS = 4
OUT_ROWS = 1024
OUT_MIN_STEPS = 4
FFN_ROWS = 1024
FFN_COLS = 512
FFN_COLS_CAST = 256
ROW_CHUNK = 512
VMEM_LIMIT = 57 * MIB


def _rms(x, gain):
    ms = jnp.mean(x * x, axis=-1, keepdims=True)
    return (x * lax.rsqrt(ms + EPS)) * gain


def _gelu_tanh(x):
    cdf = 0.5 * (1.0 + jnp.tanh(0.7978845608028654 * (x + 0.044715 * (x * x * x))))
    return x * cdf


def _sigmoid(x):
    return 0.5 + 0.5 * jnp.tanh(0.5 * x)


def _row_chunks(rows):
    step = min(ROW_CHUNK, rows)
    return [slice(r, r + step) for r in range(0, rows, step)]


def _norm_matmul_kernel(*refs, epilogues, has_bias, cast_w):
    refs = list(refs)
    h_ref = refs.pop()
    wb_ref = refs.pop() if cast_w else None
    o_ref = refs.pop()
    b_ref = refs.pop() if has_bias else None
    x_ref, g_ref, w_ref = refs
    n = pl.program_id(1)

    def block(kind, with_norm):
        if cast_w:
            wb_ref[...] = w_ref[...].astype(BF16)
        w_src = wb_ref if cast_w else w_ref
        for rs in _row_chunks(x_ref.shape[0]):
            if with_norm:
                h_ref[rs, :] = _rms(x_ref[rs, :], g_ref[...]).astype(BF16)
            z = jnp.dot(h_ref[rs, :], w_src[...], preferred_element_type=F32)
            if kind == "gelu":
                z = _gelu_tanh(z)
            elif kind == "sigmoid_bias":
                z = _sigmoid((z + b_ref[...]).astype(o_ref.dtype))
            o_ref[rs, :] = z.astype(o_ref.dtype)

    for lo, hi, kind in epilogues:
        if lo == 0:
            pl.when(n == 0)(functools.partial(block, kind, True))
            lo = 1
        if hi > lo:
            pl.when((n >= lo) & (n < hi))(functools.partial(block, kind, False))


def _norm_matmul(x2, gain, w, bias, epilogues, out_dtype, bn, bias_block0=0, cast_w=False, name="norm_matmul"):
    m, k = x2.shape
    n_cols = w.shape[1]
    bm = min(IN_ROWS, m)
    grid = (m // bm, n_cols // bn)
    assert not cast_w or grid[0] == 1, "the bf16 weight copy is written once per column block"
    in_specs = [
        pl.BlockSpec((bm, k), lambda i, j: (i, 0)),
        pl.BlockSpec((1, k), lambda i, j: (0, 0)),
        pl.BlockSpec((k, bn), lambda i, j: (0, j)),
    ]
    args = [x2, gain.reshape(1, k), w]
    if bias is not None:
        in_specs.append(pl.BlockSpec((1, bn), lambda i, j: (0, jnp.maximum(j - bias_block0, 0))))
        args.append(bias.reshape(1, -1))
    out_specs = [pl.BlockSpec((bm, bn), lambda i, j: (i, j))]
    out_shape = [jax.ShapeDtypeStruct((m, n_cols), out_dtype)]
    if cast_w:
        out_specs.append(pl.BlockSpec((k, bn), lambda i, j: (0, j)))
        out_shape.append(jax.ShapeDtypeStruct((k, n_cols), BF16))
    out = pl.pallas_call(
        functools.partial(_norm_matmul_kernel, epilogues=epilogues, has_bias=bias is not None, cast_w=cast_w),
        grid=grid,
        in_specs=in_specs,
        out_specs=out_specs,
        out_shape=out_shape,
        scratch_shapes=[pltpu.VMEM((bm, k), BF16)],
        compiler_params=pltpu.CompilerParams(
            dimension_semantics=("arbitrary", "arbitrary"),
            vmem_limit_bytes=VMEM_LIMIT),
        name=name,
    )(*args)
    return tuple(out) if cast_w else out[0]


def _in_proj(x2, gain, w, bias, cast_w, name):
    bn = IN_COLS_CAST if cast_w else IN_COLS
    epilogues = ((0, OFF_XB // bn, "gelu"), (OFF_XB // bn, OFF_GATES // bn, "identity"),
                 (OFF_GATES // bn, w.shape[1] // bn, "sigmoid_bias"))
    return _norm_matmul(x2, gain, w, bias, epilogues, BF16, bn, bias_block0=OFF_GATES // bn, cast_w=cast_w,
                        name=name)


def _mem_kv_kernel(x_ref, g_ref, wk_ref, wv_ref, k_ref, v_ref, kt_ref):
    h = _rms(x_ref[...], g_ref[...]).astype(BF16)
    k = jnp.dot(h, wk_ref[...], preferred_element_type=F32)
    k_ref[...] = k
    v_ref[...] = jnp.dot(h, wv_ref[...], preferred_element_type=F32)
    for b in range(kt_ref.shape[0]):
        for hd in range(MEM_HEADS):
            lo, hi = hd * MEM_HDIM, (hd + 1) * MEM_HDIM
            kt_ref[b, lo:hi, :] = k[b * N_MEM:(b + 1) * N_MEM, lo:hi].T.astype(BF16)


def _mem_kv(mem2, gain, wk, wv, name):
    m, d = mem2.shape
    bm = MEM_KV_BATCHES * N_MEM
    row = lambda i: (i, 0)
    const = lambda i: (0, 0)
    return pl.pallas_call(
        _mem_kv_kernel,
        grid=(m // bm,),
        in_specs=[pl.BlockSpec((bm, d), row), pl.BlockSpec((1, d), const),
                  pl.BlockSpec((d, MEM_WIDTH), const), pl.BlockSpec((d, MEM_WIDTH), const)],
        out_specs=[pl.BlockSpec((bm, MEM_WIDTH), row), pl.BlockSpec((bm, MEM_WIDTH), row),
                   pl.BlockSpec((MEM_KV_BATCHES, MEM_WIDTH, N_MEM), lambda i: (i, 0, 0))],
        out_shape=[jax.ShapeDtypeStruct((m, MEM_WIDTH), F32), jax.ShapeDtypeStruct((m, MEM_WIDTH), F32),
                   jax.ShapeDtypeStruct((m // N_MEM, MEM_WIDTH, N_MEM), BF16)],
        compiler_params=pltpu.CompilerParams(
            dimension_semantics=("arbitrary",),
            vmem_limit_bytes=VMEM_LIMIT),
        name=name,
    )(mem2, gain.reshape(1, d), wk, wv)


def _mixer_kernel(u_ref, v_ref, xb_ref, q_ref, ga_ref, gb_ref, gc_ref, pre_ref,
                  mk_ref, mv_ref, gv_ref, ws_ref, bs_ref, ps_ref, wpool_ref,
                  wpa_ref, wpb_ref, wpc_ref,
                  o_ref, sgu_scr, pool_scr, att_scr,
                  *, n_seq, seq_rows, chunk_len, pool_chunk, pos0, tiles_per_seq, zero_first_prefix):
    i = pl.program_id(0)
    tile_in_seq = i % tiles_per_seq
    tile_rows = n_seq * seq_rows

    ri = lax.broadcasted_iota(jnp.int32, (chunk_len, chunk_len), 0)
    cj = lax.broadcasted_iota(jnp.int32, (chunk_len, chunk_len), 1)
    causal = jnp.right_shift(cj, CHUNK_SHIFT) <= jnp.right_shift(ri, CHUNK_SHIFT)
    ws = [jnp.where(causal, ws_ref[g], 0.0).astype(BF16) for g in range(SGU_GROUPS)]
    bs_full = [jnp.broadcast_to(bs_ref[:, g:g + 1], (chunk_len, SGU_GDIM)) for g in range(SGU_GROUPS)]

    bt = lax.broadcasted_iota(jnp.int32, (pool_chunk, 2 * pool_chunk), 0)
    bj = lax.broadcasted_iota(jnp.int32, (pool_chunk, 2 * pool_chunk), 1)
    back = bt + pool_chunk - bj
    bands = [jnp.where(back >= 0, jnp.where(back < w, 1.0, 0.0), 0.0).astype(BF16) for w in POOL_WINDOWS]
    trow = lax.broadcasted_iota(jnp.int32, (pool_chunk, 1), 0)

    keep_prefix = jnp.where(tile_in_seq == 0, 0.0, 1.0) if zero_first_prefix else None

    for s in range(n_seq):
        r0 = s * seq_rows

        kb = mk_ref[s].astype(BF16)
        vb = mv_ref[s].astype(BF16)
        for h in range(MEM_HEADS):
            lo, hi = h * MEM_HDIM, (h + 1) * MEM_HDIM
            qh = q_ref[r0:r0 + seq_rows, lo:hi]
            sc = jnp.dot(qh, kb[lo:hi, :], preferred_element_type=F32) * (MEM_HDIM ** -0.5)
            e = jnp.exp(sc - jnp.max(sc, axis=-1, keepdims=True))
            p = (e / jnp.sum(e, axis=-1, keepdims=True)).astype(BF16)
            att_scr[r0:r0 + seq_rows, lo:hi] = jnp.dot(
                p, vb[:, lo:hi], preferred_element_type=F32).astype(BF16)

    assert chunk_len == pool_chunk
    chunk_rows = [(s, c, s * seq_rows + c * chunk_len) for s in range(n_seq) for c in range(seq_rows // chunk_len)]

    vn = [_rms(v_ref[s * seq_rows:(s + 1) * seq_rows, :].astype(F32), gv_ref[...]).astype(BF16)
          for s in range(n_seq)]
    for g in range(SGU_GROUPS):
        lo, hi = g * SGU_GDIM, (g + 1) * SGU_GDIM
        v_side = jnp.concatenate([vn[s][c * chunk_len:(c + 1) * chunk_len, lo:hi] for s, c, _ in chunk_rows], axis=1)
        mixed = jnp.dot(ws[g], v_side, preferred_element_type=F32)
        for k, (_, _, r) in enumerate(chunk_rows):
            u = u_ref[r:r + chunk_len, lo:hi].astype(F32)
            mixed_k = mixed[:, k * SGU_GDIM:(k + 1) * SGU_GDIM] + bs_full[g]
            sgu_scr[r:r + chunk_len, lo:hi] = (u * mixed_k).astype(BF16)

    for g, w in enumerate(POOL_WINDOWS):
        lo, hi = g * POOL_GDIM, (g + 1) * POOL_GDIM
        windows = []
        for s, c, r in chunk_rows:
            if c == 0:
                prev = pre_ref[s * pool_chunk:(s + 1) * pool_chunk, lo:hi]
                if zero_first_prefix:
                    prev = (prev.astype(F32) * keep_prefix).astype(BF16)
            else:
                prev = xb_ref[r - pool_chunk:r, lo:hi]
            windows.append(jnp.concatenate([prev, xb_ref[r:r + pool_chunk, lo:hi]], axis=0))
        wsum = jnp.dot(bands[g], jnp.concatenate(windows, axis=1), preferred_element_type=F32)
        pooled = []
        for k, (_, c, r) in enumerate(chunk_rows):
            pos = pos0 + tile_in_seq * tile_rows + c * pool_chunk + trow
            cnt = jnp.minimum(pos + 1, w).astype(F32)
            cur = xb_ref[r:r + pool_chunk, lo:hi].astype(F32)
            pooled.append((wsum[:, k * POOL_GDIM:(k + 1) * POOL_GDIM] / cnt - cur).astype(BF16))
        pg = jnp.dot(jnp.concatenate(pooled, axis=0), wpool_ref[g], preferred_element_type=F32)
        pool_scr[:, lo:hi] = (pg * ps_ref[:, lo:hi]).astype(BF16)

    merged = ga_ref[...].astype(F32) * jnp.dot(sgu_scr[...], wpa_ref[...], preferred_element_type=F32)
    merged += gb_ref[...].astype(F32) * jnp.dot(pool_scr[...], wpb_ref[...], preferred_element_type=F32)
    merged += gc_ref[...].astype(F32) * jnp.dot(att_scr[...], wpc_ref[...], preferred_element_type=F32)
    o_ref[...] = merged.astype(BF16)


def _mixer(z, prefix, prefix_in_z, mem_k, mem_v, gv, ws, bs_t, pscale, wpool, wpa, wpb, wpc,
           *, n_seq, seq_rows, chunk_len, pos0, tiles_per_seq, name):
    m = z.shape[0]
    d = wpa.shape[1]
    tm = n_seq * seq_rows
    pool_chunk = min(SGU_CHUNK, seq_rows)
    grid = (m // tm,)
    const2 = lambda i: (0, 0)
    const3 = lambda i: (0, 0, 0)
    single = pl.Buffered(1)
    gate_block0 = OFF_GATES // d
    if prefix_in_z:
        blocks_per_tile = tm // pool_chunk
        pre_spec = pl.BlockSpec((pool_chunk, POOL_WIDTH),
                                lambda i: (jnp.maximum(i * blocks_per_tile - 1, 0), OFF_XB // POOL_WIDTH))
        mem_idx = lambda i: (i // tiles_per_seq, 0, 0)
    else:
        pre_spec = pl.BlockSpec((n_seq * pool_chunk, POOL_WIDTH), lambda i: (i, 0))
        mem_idx = lambda i: (i, 0, 0)
    in_specs = [
        pl.BlockSpec((tm, SGU_WIDTH), lambda i: (i, 0)),
        pl.BlockSpec((tm, SGU_WIDTH), lambda i: (i, 1)),
        pl.BlockSpec((tm, POOL_WIDTH), lambda i: (i, 2)),
        pl.BlockSpec((tm, MEM_WIDTH), lambda i: (i, 3)),
        pl.BlockSpec((tm, d), lambda i: (i, gate_block0)),
        pl.BlockSpec((tm, d), lambda i: (i, gate_block0 + 1)),
        pl.BlockSpec((tm, d), lambda i: (i, gate_block0 + 2)),
        pre_spec,
        pl.BlockSpec((n_seq, MEM_WIDTH, N_MEM), mem_idx),
        pl.BlockSpec((n_seq, N_MEM, MEM_WIDTH), mem_idx),
        pl.BlockSpec((1, SGU_WIDTH), const2),
        pl.BlockSpec((SGU_GROUPS, chunk_len, chunk_len), const3),
        pl.BlockSpec((chunk_len, SGU_GROUPS), const2),
        pl.BlockSpec((1, POOL_WIDTH), const2),
        pl.BlockSpec((len(POOL_WINDOWS), POOL_GDIM, POOL_GDIM), const3, pipeline_mode=single),
        pl.BlockSpec((SGU_WIDTH, d), const2, pipeline_mode=single),
        pl.BlockSpec((POOL_WIDTH, d), const2, pipeline_mode=single),
        pl.BlockSpec((MEM_WIDTH, d), const2, pipeline_mode=single),
    ]
    kern = functools.partial(
        _mixer_kernel, n_seq=n_seq, seq_rows=seq_rows, chunk_len=chunk_len, pool_chunk=pool_chunk,
        pos0=pos0, tiles_per_seq=tiles_per_seq, zero_first_prefix=prefix_in_z)
    return pl.pallas_call(
        kern,
        grid=grid,
        in_specs=in_specs,
        out_specs=pl.BlockSpec((tm, d), lambda i: (i, 0)),
        out_shape=jax.ShapeDtypeStruct((m, d), BF16),
        scratch_shapes=[pltpu.VMEM((tm, SGU_WIDTH), BF16),
                        pltpu.VMEM((tm, POOL_WIDTH), BF16),
                        pltpu.VMEM((tm, MEM_WIDTH), BF16)],
        compiler_params=pltpu.CompilerParams(
            dimension_semantics=("arbitrary",),
            vmem_limit_bytes=VMEM_LIMIT),
        name=name,
    )(z, z, z, z, z, z, z, prefix, mem_k, mem_v, gv.reshape(1, -1), ws, bs_t,
      pscale.reshape(1, -1), wpool, wpa, wpb, wpc)


def _out_proj_kernel(x_ref, m_ref, w_ref, o_ref):
    for rs in _row_chunks(x_ref.shape[0]):
        o_ref[rs, :] = x_ref[rs, :] + jnp.dot(m_ref[rs, :], w_ref[...], preferred_element_type=F32)


def _out_proj(x2, merged, wo, name):
    m, d = x2.shape
    tm = min(OUT_ROWS, m // OUT_MIN_STEPS)
    row = lambda i: (i, 0)
    return pl.pallas_call(
        _out_proj_kernel,
        grid=(m // tm,),
        in_specs=[pl.BlockSpec((tm, d), row), pl.BlockSpec((tm, d), row),
                  pl.BlockSpec((d, d), lambda i: (0, 0), pipeline_mode=pl.Buffered(1))],
        out_specs=pl.BlockSpec((tm, d), row),
        out_shape=jax.ShapeDtypeStruct((m, d), F32),
        compiler_params=pltpu.CompilerParams(
            dimension_semantics=("arbitrary",),
            vmem_limit_bytes=VMEM_LIMIT),
        name=name,
    )(x2, merged, wo)


def _ffn_kernel(*refs, n_chunks, cast_w):
    refs = list(refs)
    h_ref = refs.pop()
    if cast_w:
        wdb_ref, wub_ref, wgb_ref = refs.pop(), refs.pop(), refs.pop()
    o_ref = refs.pop()
    x_ref, g_ref, wg_ref, wu_ref, wd_ref, gf_ref = refs
    f = pl.program_id(1)

    if cast_w:
        wgb_ref[...] = wg_ref[...].astype(BF16)
        wub_ref[...] = wu_ref[...].astype(BF16)
        wdb_ref[...] = wd_ref[...].astype(BF16)
        wg_ref, wu_ref, wd_ref = wgb_ref, wub_ref, wdb_ref

    def step(first, last):
        for rs in _row_chunks(x_ref.shape[0]):
            if first:
                h_ref[rs, :] = _rms(x_ref[rs, :], g_ref[...]).astype(BF16)
            h = h_ref[rs, :]
            gate = jnp.dot(h, wg_ref[...], preferred_element_type=F32)
            up = jnp.dot(h, wu_ref[...], preferred_element_type=F32)
            act = ((gate * _sigmoid(gate)) * up).astype(BF16)
            acc = x_ref[rs, :] if first else o_ref[rs, :]
            y = acc + jnp.dot(act, wd_ref[...], preferred_element_type=F32)
            o_ref[rs, :] = _rms(y, gf_ref[...]) if last else y

    pl.when(f == 0)(functools.partial(step, True, False))
    pl.when((f > 0) & (f < n_chunks - 1))(functools.partial(step, False, False))
    pl.when(f == n_chunks - 1)(functools.partial(step, False, True))


def _ffn(x2, g_ffn, wg, wu, wd, g_final, *, cast_w, name):
    m, d = x2.shape
    d_ff = wg.shape[1]
    tm = min(FFN_ROWS, m)
    assert not cast_w or m == tm, "the bf16 weight copies are written once per d_ff chunk"
    fc = FFN_COLS_CAST if cast_w else FFN_COLS
    n_chunks = d_ff // fc
    row = lambda i, f: (i, 0)
    vec = lambda i, f: (0, 0)
    cols = pl.BlockSpec((d, fc), lambda i, f: (0, f))
    rows = pl.BlockSpec((fc, d), lambda i, f: (f, 0))
    out_specs = [pl.BlockSpec((tm, d), row)]
    out_shape = [jax.ShapeDtypeStruct((m, d), F32)]
    if cast_w:
        out_specs += [cols, cols, rows]
        out_shape += [jax.ShapeDtypeStruct(w.shape, BF16) for w in (wg, wu, wd)]
    out = pl.pallas_call(
        functools.partial(_ffn_kernel, n_chunks=n_chunks, cast_w=cast_w),
        grid=(m // tm, n_chunks),
        in_specs=[pl.BlockSpec((tm, d), row), pl.BlockSpec((1, d), vec), cols, cols, rows,
                  pl.BlockSpec((1, d), vec)],
        out_specs=out_specs,
        out_shape=out_shape,
        scratch_shapes=[pltpu.VMEM((tm, d), BF16)],
        compiler_params=pltpu.CompilerParams(
            dimension_semantics=("arbitrary", "arbitrary"),
            vmem_limit_bytes=VMEM_LIMIT),
        name=name,
    )(x2, g_ffn.reshape(1, d), wg, wu, wd, g_final.reshape(1, d))
    return tuple(out) if cast_w else out[0]


def kernel(x_prompt, x_sample, mem_prompt, state_pool, cache_mem_k, cache_mem_v, g_mix, w_in, b_gate, g_sgu_v, w_sgu, b_sgu, w_pool, pool_scale, g_mem, w_mk, w_mv, w_pa, w_pb, w_pc, w_o, g_ffn, w_ff_gate, w_ff_up, w_ff_down, g_final):
    depth = g_mix.shape[0]
    batch, seq, d = x_prompt.shape
    dec_batch, dec_seq, _ = x_sample.shape
    assert depth == 1, "final norm is fused into the last layer's FFN call"
    assert d == IN_COLS and dec_seq >= POOL_STATE

    xp = x_prompt.reshape(batch * seq, d)
    xs = x_sample.reshape(dec_batch * dec_seq, d)
    mem2 = mem_prompt.reshape(batch * N_MEM, d)

    pool_p, pool_s, mk_p, mv_p, v_s = [], [], [], [], []
    for l in range(depth):
        wpool_b = w_pool[l].astype(BF16)
        wpa_b, wpb_b, wpc_b, wo_b = (w[l].astype(BF16) for w in (w_pa, w_pb, w_pc, w_o))
        branch_weights = (wpool_b, wpa_b, wpb_b, wpc_b)
        gv, ws, bs, pscale = g_sgu_v[l], w_sgu[l], b_sgu[l], pool_scale[l]

        zs, w_in_b = _in_proj(xs, g_mix[l], w_in[l], b_gate[l], True, "in_proj_sample")
        prefix = jnp.pad(state_pool[l], ((0, 0), (dec_seq - POOL_STATE, 0), (0, 0)))
        prefix = prefix.reshape(dec_batch * dec_seq, POOL_WIDTH).astype(BF16)
        ms = _mixer(zs, prefix, False,
                    jnp.transpose(cache_mem_k[l], (0, 2, 3, 1)).reshape(dec_batch, MEM_WIDTH, N_MEM).astype(BF16),
                    cache_mem_v[l].reshape(dec_batch, N_MEM, MEM_WIDTH),
                    gv, ws[:, :dec_seq, :dec_seq], jnp.transpose(bs[:, :dec_seq]), pscale, *branch_weights,
                    n_seq=min(MIXER_SAMPLE_SEQS, dec_batch), seq_rows=dec_seq, chunk_len=dec_seq,
                    pos0=PAST_LEN, tiles_per_seq=1, name="mixer_sample")
        x1s = _out_proj(xs, ms, wo_b, "out_proj_sample")
        xs, wg_b, wu_b, wd_b = _ffn(x1s, g_ffn[l], w_ff_gate[l], w_ff_up[l], w_ff_down[l], g_final,
                                    cast_w=True, name="ffn_sample")
        zs3 = zs.reshape(dec_batch, dec_seq, -1)
        pool_s.append(zs3[:, dec_seq - POOL_STATE:, OFF_XB:OFF_XB + POOL_WIDTH].astype(F32))
        v_s.append(zs3[:, :, SGU_WIDTH:2 * SGU_WIDTH].astype(F32))

        mk, mv, mkt = _mem_kv(mem2, g_mem[l], w_mk[l].astype(BF16), w_mv[l].astype(BF16), "mem_kv")
        mk_p.append(mk.reshape(batch, N_MEM, MEM_HEADS, MEM_HDIM))
        mv_p.append(mv.reshape(batch, N_MEM, MEM_HEADS, MEM_HDIM))

        zp = _in_proj(xp, g_mix[l], w_in_b, b_gate[l], False, "in_proj_prompt")
        tm_p = min(MIXER_ROWS, seq)
        mp = _mixer(zp, zp, True, mkt, mv.reshape(batch, N_MEM, MEM_WIDTH),
                    gv, ws, jnp.transpose(bs), pscale, *branch_weights,
                    n_seq=1, seq_rows=tm_p, chunk_len=SGU_CHUNK, pos0=0, tiles_per_seq=seq // tm_p,
                    name="mixer_prompt")
        x1p = _out_proj(xp, mp, wo_b, "out_proj_prompt")
        xp = _ffn(x1p, g_ffn[l], wg_b, wu_b, wd_b, g_final, cast_w=False, name="ffn_prompt")
        xb_p = zp.reshape(batch, seq, -1)[:, seq - POOL_STATE:, OFF_XB:OFF_XB + POOL_WIDTH]
        pool_p.append(xb_p.astype(F32))

    y_prompt = xp.reshape(batch, seq, d)
    y_sample = xs.reshape(dec_batch, dec_seq, d)
    return (y_prompt, y_sample, jnp.stack(pool_p), jnp.stack(pool_s),
            jnp.stack(mk_p), jnp.stack(mv_p), jnp.stack(v_s))
```
